```python
import math
import jax, jax.numpy as jnp
from jax import lax
import numpy as np

D_MODEL = 1024
BATCH = 4
SEQ = 4096
DEPTH = 4
DEC_BATCH = 32
DEC_SEQ = 4
PAST_LEN = 8192
PAGE_SIZE = 128

N_HEADS_RET = 4
HD_RET = 128
D_RET = N_HEADS_RET * HD_RET
N_HEADS_ATT = 8
HD_ATT = 64
D_ATT = N_HEADS_ATT * HD_ATT
D_MIX = D_RET + D_ATT
D_IN = 4 * D_RET + 3 * D_ATT
SPLITS = (D_RET, 2 * D_RET, 3 * D_RET, 4 * D_RET, 4 * D_RET + D_ATT, 4 * D_RET + 2 * D_ATT)
DIL_PATTERNS = ((128, 1), (512, 4), (2048, 16))
WIN_MAX = 2048
RET_CHUNK = 128
D_FF = 2816
CONV_W = 3
ROPE_THETA = 10000.0
EPS = 1e-6

kernel_name = "hybrid_retention_dilated_attn_convffn_step"


def rms_norm(x, w):
    xf = x.astype(jnp.float32)
    y = xf * lax.rsqrt(jnp.mean(xf * xf, axis=-1, keepdims=True) + EPS)
    return (y * w.astype(jnp.float32)).astype(x.dtype)


def rotary(x, pos):
    half = x.shape[-1] // 2
    inv = ROPE_THETA ** (-jnp.arange(half, dtype=jnp.float32) / half)
    ang = pos.astype(jnp.float32)[:, None] * inv[None, :]
    cos = jnp.cos(ang)[None, :, None, :]
    sin = jnp.sin(ang)[None, :, None, :]
    xf = x.astype(jnp.float32)
    x1, x2 = xf[..., :half], xf[..., half:]
    return jnp.concatenate([x1 * cos - x2 * sin, x2 * cos + x1 * sin], axis=-1).astype(x.dtype)


def ret_log_decay():
    return jnp.log1p(-jnp.exp2(-5.0 - jnp.arange(N_HEADS_RET, dtype=jnp.float32)))


def retention(q, k, v, s0):
    C = q.shape[2]
    lg = ret_log_decay()
    idx = jnp.arange(C, dtype=jnp.float32)
    diff = idx[:, None] - idx[None, :]
    causal = diff >= 0
    dmask = jnp.where(causal, jnp.exp(jnp.where(causal, diff, 0.0)[None] * lg[:, None, None]), 0.0)
    k = k * HD_RET ** -0.5
    scores = jnp.einsum('bnihd,bnjhd->bnhij', q, k) * dmask
    inner = jnp.einsum('bnhij,bnjhe->bnihe', scores, v)
    kdec = k * jnp.exp((C - 1.0 - idx)[:, None] * lg[None, :])[None, None, :, :, None]
    upd = jnp.einsum('bnjhd,bnjhe->nbhde', kdec, v)
    chunk_decay = jnp.exp(C * lg)[None, :, None, None]

    def step(s, u):
        return chunk_decay * s + u, s

    s_final, s_before = lax.scan(step, s0, upd)
    qdec = q * jnp.exp((idx + 1.0)[:, None] * lg[None, :])[None, None, :, :, None]
    cross = jnp.einsum('bnihd,nbhde->bnihe', qdec, s_before)
    return inner + cross, s_final


def combine_by_denominator(outs, lses):
    w = jax.nn.softmax(jnp.stack(lses, axis=0), axis=0)
    return jnp.sum(w[..., None] * jnp.stack(outs, axis=0), axis=0)


def dilated_attn_prompt(q, k, v):
    B, S, H, hd = q.shape
    outs, lses = [], []
    for window, dil in DIL_PATTERNS:
        nb = window // dil
        span = dil * nb
        Sp = -(-S // span) * span
        nblk = Sp // span
        pad = ((0, 0), (0, Sp - S), (0, 0), (0, 0))

        def strided(t):
            t = jnp.pad(t.astype(jnp.float32), pad).reshape(B, Sp // dil, dil, H, hd)
            return t.transpose(0, 2, 1, 3, 4).reshape(B, dil, nblk, nb, H, hd)

        def with_prev(t):
            prev = jnp.pad(t, ((0, 0), (0, 0), (1, 0), (0, 0), (0, 0), (0, 0)))[:, :, :-1]
            return jnp.concatenate([prev, t], axis=3)

        qs = strided(q)
        kb, vb = with_prev(strided(k)), with_prev(strided(v))
        s = jnp.einsum('brnqhd,brnkhd->brnhqk', qs, kb) * hd ** -0.5
        qi = jnp.arange(nb)[:, None]
        ki = jnp.arange(2 * nb)[None, :] - nb
        dist = qi - ki
        blk = jnp.arange(nblk)[:, None, None]
        valid = (dist >= 0) & (dist <= nb) & (blk * nb + ki >= 0)
        s = jnp.where(valid[None, None, :, None], s, -jnp.inf)
        m = jnp.max(s, axis=-1, keepdims=True)
        p = jnp.exp(s - m)
        den = jnp.sum(p, axis=-1, keepdims=True)
        o = jnp.einsum('brnhqk,brnkhd->brnqhd', p / den, vb)
        lse = (m + jnp.log(den))[..., 0]
        o = o.reshape(B, dil, Sp // dil, H, hd).transpose(0, 2, 1, 3, 4).reshape(B, Sp, H, hd)[:, :S]
        lse = lse.transpose(0, 1, 2, 4, 3).reshape(B, dil, Sp // dil, H).transpose(0, 2, 1, 3).reshape(B, Sp, H)[:, :S]
        outs.append(o)
        lses.append(lse)
    return combine_by_denominator(outs, lses)


def dilated_attn_sample(q, k_all, v_all, n_ctx):
    B, T, H, hd = q.shape
    qf = q.astype(jnp.float32)
    kf, vf = k_all.astype(jnp.float32), v_all.astype(jnp.float32)
    outs, lses = [], []
    for window, dil in DIL_PATTERNS:
        n = window // dil + 1
        idx = n_ctx + jnp.arange(T)[:, None] - dil * jnp.arange(n)[None, :]
        valid = idx >= 0
        idx_c = jnp.clip(idx, 0, None)
        kg = jnp.take(kf, idx_c, axis=1)
        vg = jnp.take(vf, idx_c, axis=1)
        s = jnp.einsum('bthd,btnhd->btnh', qf, kg) * hd ** -0.5
        s = jnp.where(valid[None, :, :, None], s, -jnp.inf)
        m = jnp.max(s, axis=2, keepdims=True)
        p = jnp.exp(s - m)
        den = jnp.sum(p, axis=2, keepdims=True)
        outs.append(jnp.einsum('btnh,btnhd->bthd', p / den, vg))
        lses.append((m + jnp.log(den))[:, :, 0, :])
    return combine_by_denominator(outs, lses)


def project(h, w_in_l, pos):
    B, T, _ = h.shape
    z = h @ w_in_l
    qr, kr, vr, gr, qa, ka, va = jnp.split(z, SPLITS, axis=-1)
    hr = lambda t: t.reshape(B, T, N_HEADS_RET, HD_RET)
    ha = lambda t: t.reshape(B, T, N_HEADS_ATT, HD_ATT)
    qr, kr = rotary(hr(qr), pos), rotary(hr(kr), pos)
    qa, ka = rotary(ha(qa), pos), rotary(ha(ka), pos)
    return qr, kr, hr(vr), gr, qa, ka, ha(va)


def merge(o_ret, g_ret, o_att, gn_w_l, w_out_l, dtype):
    B, T = o_ret.shape[:2]
    mu = jnp.mean(o_ret, axis=-1, keepdims=True)
    var = jnp.mean(jnp.square(o_ret - mu), axis=-1, keepdims=True)
    y = ((o_ret - mu) * lax.rsqrt(var + EPS)).reshape(B, T, D_RET) * gn_w_l.astype(jnp.float32)
    y = jax.nn.silu(g_ret.astype(jnp.float32)) * y
    cat = jnp.concatenate([y.astype(dtype), o_att.reshape(B, T, D_ATT).astype(dtype)], axis=-1)
    return cat @ w_out_l


def token_mixer_prompt(h, w_in_l, gn_w_l, w_out_l):
    B, S, _ = h.shape
    pos = jnp.arange(S, dtype=jnp.int32)
    qr, kr, vr, gr, qa, ka, va = project(h, w_in_l, pos)
    C = RET_CHUNK
    chunk = lambda t: t.astype(jnp.float32).reshape(B, S // C, C, N_HEADS_RET, HD_RET)
    s0 = jnp.zeros((B, N_HEADS_RET, HD_RET, HD_RET), jnp.float32)
    o_r, s_fin = retention(chunk(qr), chunk(kr), chunk(vr), s0)
    o_r = o_r.reshape(B, S, N_HEADS_RET, HD_RET)
    o_a = dilated_attn_prompt(qa, ka, va)
    out = merge(o_r, gr, o_a, gn_w_l, w_out_l, h.dtype)
    w_keep = min(WIN_MAX, S)
    return out, ka[:, S - w_keep:], va[:, S - w_keep:], s_fin


def token_mixer_sample(h, ck, cv, s0, w_in_l, gn_w_l, w_out_l):
    B, T, _ = h.shape
    pos = PAST_LEN + jnp.arange(T, dtype=jnp.int32)
    qr, kr, vr, gr, qa, ka, va = project(h, w_in_l, pos)
    one = lambda t: t.astype(jnp.float32).reshape(B, 1, T, N_HEADS_RET, HD_RET)
    o_r, s_new = retention(one(qr), one(kr), one(vr), s0.astype(jnp.float32))
    o_r = o_r.reshape(B, T, N_HEADS_RET, HD_RET)
    k_all = jnp.concatenate([ck.astype(ka.dtype), ka], axis=1)
    v_all = jnp.concatenate([cv.astype(va.dtype), va], axis=1)
    o_a = dilated_attn_sample(qa, k_all, v_all, ck.shape[1])
    out = merge(o_r, gr, o_a, gn_w_l, w_out_l, h.dtype)
    return out, ka, va, s_new


def conv_ffn(h, ctx, w_up_l, conv_w_l, conv_b_l, w_down_l):
    u = h @ w_up_l
    T = u.shape[1]
    ext = jnp.concatenate([ctx.astype(u.dtype), u], axis=1)
    c = conv_b_l
    for j in range(CONV_W):
        c = c + ext[:, j:j + T] * conv_w_l[j]
    g, val = c[..., :D_FF], c[..., D_FF:]
    y = (jax.nn.silu(g) * val) @ w_down_l
    return y, ext[:, ext.shape[1] - (CONV_W - 1):]


def setup_inputs(seed: int = 0) -> dict:
    key = jax.random.key(seed)
    ks = jax.random.split(key, 20)
    f32 = jnp.float32
    nrm = lambda k, shape, scale: scale * jax.random.normal(k, shape, f32)
    w_ctx = min(WIN_MAX, PAST_LEN)
    return {
        "x_prompt": nrm(ks[0], (BATCH, SEQ, D_MODEL), 1.0),
        "x_sample": nrm(ks[1], (DEC_BATCH, DEC_SEQ, D_MODEL), 1.0),
        "cache_win_k": nrm(ks[2], (DEPTH, DEC_BATCH, w_ctx, N_HEADS_ATT, HD_ATT), 1.0),
        "cache_win_v": nrm(ks[3], (DEPTH, DEC_BATCH, w_ctx, N_HEADS_ATT, HD_ATT), 1.0),
        "state_ret": nrm(ks[4], (DEPTH, DEC_BATCH, N_HEADS_RET, HD_RET, HD_RET), 0.5),
        "state_conv": nrm(ks[5], (DEPTH, DEC_BATCH, CONV_W - 1, 2 * D_FF), 0.5),
        "norm1_w": 1.0 + nrm(ks[6], (DEPTH, D_MODEL), 0.02),
        "w_in": nrm(ks[7], (DEPTH, D_MODEL, D_IN), D_MODEL ** -0.5),
        "ret_gn_w": 1.0 + nrm(ks[8], (DEPTH, D_RET), 0.02),
        "w_out": nrm(ks[9], (DEPTH, D_MIX, D_MODEL), D_MIX ** -0.5),
        "norm2_w": 1.0 + nrm(ks[10], (DEPTH, D_MODEL), 0.02),
        "w_up": nrm(ks[11], (DEPTH, D_MODEL, 2 * D_FF), D_MODEL ** -0.5),
        "conv_w": nrm(ks[12], (DEPTH, CONV_W, 2 * D_FF), CONV_W ** -0.5),
        "conv_b": nrm(ks[13], (DEPTH, 2 * D_FF), 0.02),
        "w_down": nrm(ks[14], (DEPTH, D_FF, D_MODEL), D_FF ** -0.5),
        "final_norm_w": 1.0 + nrm(ks[15], (D_MODEL,), 0.02),
    }


def reference(x_prompt, x_sample, cache_win_k, cache_win_v, state_ret, state_conv,
              norm1_w, w_in, ret_gn_w, w_out, norm2_w, w_up, conv_w, conv_b, w_down, final_norm_w):
    xp, xs = x_prompt, x_sample
    B = xp.shape[0]
    pk, pv, pr, pc, sk, sv, sr, sc = [], [], [], [], [], [], [], []
    for l in range(DEPTH):
        mp, kp_, vp_, rp_ = token_mixer_prompt(rms_norm(xp, norm1_w[l]), w_in[l], ret_gn_w[l], w_out[l])
        xp = xp + mp
        ms, ks_, vs_, rs_ = token_mixer_sample(rms_norm(xs, norm1_w[l]), cache_win_k[l], cache_win_v[l],
                                              state_ret[l], w_in[l], ret_gn_w[l], w_out[l])
        xs = xs + ms
        zero_ctx = jnp.zeros((B, CONV_W - 1, 2 * D_FF), xp.dtype)
        fp, cp_ = conv_ffn(rms_norm(xp, norm2_w[l]), zero_ctx, w_up[l], conv_w[l], conv_b[l], w_down[l])
        xp = xp + fp
        fs, cs_ = conv_ffn(rms_norm(xs, norm2_w[l]), state_conv[l], w_up[l], conv_w[l], conv_b[l], w_down[l])
        xs = xs + fs
        pk.append(kp_); pv.append(vp_); pr.append(rp_); pc.append(cp_)
        sk.append(ks_); sv.append(vs_); sr.append(rs_); sc.append(cs_)
    y_prompt = rms_norm(xp, final_norm_w)
    y_sample = rms_norm(xs, final_norm_w)
    p_win_k, p_win_v = jnp.stack(pk), jnp.stack(pv)
    p_ret, p_conv = jnp.stack(pr), jnp.stack(pc)
    s_win_k, s_win_v = jnp.stack(sk), jnp.stack(sv)
    s_ret, s_conv = jnp.stack(sr), jnp.stack(sc)
    return (y_prompt, y_sample, p_win_k, p_win_v, p_ret, p_conv, s_win_k, s_win_v, s_ret, s_conv)
```

```python
import functools

import jax
import jax.numpy as jnp
from jax import lax
from jax.experimental import pallas as pl
from jax.experimental.pallas import tpu as pltpu

F32 = jnp.float32
BF16 = jnp.bfloat16

PAST_LEN = 8192
N_HEADS_RET = 4
HD_RET = 128
N_HEADS_ATT = 8
HD_ATT = 64
D_RET = N_HEADS_RET * HD_RET
D_ATT = N_HEADS_ATT * HD_ATT
DIL_PATTERNS = ((128, 1), (512, 4), (2048, 16))
RET_CHUNK = 128
ROPE_THETA = 10000.0
EPS = 1e-6

LANES = 128
ATT_BLOCK = 128
SEC = 512
VMEM_LIMIT = 56 * 1024 * 1024


def _cparams(n_axes):
    return pltpu.CompilerParams(dimension_semantics=("arbitrary",) * n_axes,
                                vmem_limit_bytes=VMEM_LIMIT)


def _const_spec(shape):
    zeros = (0,) * len(shape)
    return pl.BlockSpec(shape, lambda *_: zeros, pipeline_mode=pl.Buffered(1))


def _rms(x, w):
    ms = jnp.mean(x * x, axis=-1, keepdims=True)
    return x * lax.rsqrt(ms + EPS) * w


def _silu(g):
    return g * (1.0 / (1.0 + jnp.exp(-g)))


def _dot(a, b):
    return jnp.dot(a, b, preferred_element_type=F32)


def _dot_nt(a, b):
    return lax.dot_general(a, b, (((1,), (1,)), ((), ())), preferred_element_type=F32)


def _dot_tn(a, b):
    return lax.dot_general(a, b, (((0,), (0,)), ((), ())), preferred_element_type=F32)


def _in_proj_body(x_ref, nw_ref, w_ref, cr_ref, sr_ref, ca_ref, sa_ref,
                  qr_ref, kr_ref, vr_ref, gr_ref, qa_ref, ka_ref, va_ref, wk_ref, wv_ref,
                  *, tiles_per_seq, win_tiles):
    h = _rms(x_ref[...], nw_ref[...]).astype(BF16)
    cr, sr, ca, sa = cr_ref[...], sr_ref[...], ca_ref[...], sa_ref[...]
    lane = lax.broadcasted_iota(jnp.int32, cr.shape, 1)
    first_half = (lane % HD_ATT) < (HD_ATT // 2)
    in_window = (pl.program_id(0) % tiles_per_seq) >= (tiles_per_seq - win_tiles)

    def sec(s):
        return _dot(h, w_ref[:, s * SEC:(s + 1) * SEC])

    def rot_ret(z):
        return z * cr + pltpu.roll(z, HD_RET // 2, 1) * sr

    def rot_att(z):
        partner = jnp.where(first_half, pltpu.roll(z, LANES - HD_ATT // 2, 1),
                            pltpu.roll(z, HD_ATT // 2, 1))
        return z * ca + partner * sa

    def emit(s, out_ref, fn, win_ref=None):
        z = sec(s)
        for c in range(SEC // LANES):
            cs = slice(c * LANES, (c + 1) * LANES)
            v = fn(z[:, cs])
            out_ref[:, cs] = v.astype(out_ref.dtype)
            if win_ref is not None:
                @pl.when(in_window)
                def _():
                    win_ref[:, cs] = v

    ident = lambda z: z
    emit(0, qr_ref, rot_ret)
    emit(1, kr_ref, rot_ret)
    emit(2, vr_ref, ident)
    emit(3, gr_ref, ident)
    emit(4, qa_ref, lambda z: rot_att(z) * (HD_ATT ** -0.5))
    emit(5, ka_ref, rot_att, wk_ref)
    emit(6, va_ref, ident, wv_ref)


def _in_proj(x, nw, w, tabs, *, tm, tiles_per_seq, win_tiles, act_dtype):
    n, d = x.shape
    n_tiles = n // tm
    n_seq = n_tiles // tiles_per_seq
    tab_tiles = tabs[0].shape[0] // tm
    row = lambda i: (i, 0)
    tab = lambda i: (i % tab_tiles, 0)
    first_win = tiles_per_seq - win_tiles
    win = lambda i: ((i // tiles_per_seq) * win_tiles + jnp.maximum(i % tiles_per_seq - first_win, 0), 0)
    act = jax.ShapeDtypeStruct((n, SEC), act_dtype)
    wshape = jax.ShapeDtypeStruct((n_seq * win_tiles * tm, SEC), F32)
    return pl.pallas_call(
        functools.partial(_in_proj_body, tiles_per_seq=tiles_per_seq, win_tiles=win_tiles),
        grid=(n_tiles,),
        in_specs=[pl.BlockSpec((tm, d), row), _const_spec((1, d)), _const_spec(w.shape)]
                 + [pl.BlockSpec((tm, LANES), tab)] * 4,
        out_specs=[pl.BlockSpec((tm, SEC), row)] * 7 + [pl.BlockSpec((tm, SEC), win)] * 2,
        out_shape=[act] * 7 + [wshape] * 2,
        compiler_params=_cparams(1),
        name="in_proj",
    )(x, nw, w, *tabs)


def _gn_gate(o, g, gn_w):
    mu = jnp.mean(o, axis=-1, keepdims=True)
    d = o - mu
    var = jnp.mean(d * d, axis=-1, keepdims=True)
    y = d * lax.rsqrt(var + EPS) * gn_w
    return _silu(g.astype(F32)) * y


def _ret_prompt_body(q_ref, k_ref, v_ref, g_ref, gn_ref, dm_ref, qd_ref, kd_ref, cd_ref,
                     y_ref, sfin_ref, s_scr):
    c = pl.program_id(1)

    @pl.when(c == 0)
    def _():
        s_scr[...] = jnp.zeros_like(s_scr)

    for h in range(N_HEADS_RET):
        sl = slice(h * HD_RET, (h + 1) * HD_RET)
        q, k, v = q_ref[:, sl], k_ref[:, sl], v_ref[:, sl]
        s = _dot_nt(q, k) * dm_ref[h]
        inner = _dot(s.astype(BF16), v)
        state = s_scr[h]
        cross = _dot(q, state.astype(BF16)) * qd_ref[h]
        kd = (k.astype(F32) * kd_ref[h]).astype(BF16)
        s_scr[h] = cd_ref[h] * state + _dot_tn(kd, v)
        y_ref[:, sl] = _gn_gate(inner + cross, g_ref[:, sl], gn_ref[:, sl]).astype(y_ref.dtype)

    @pl.when(c == pl.num_programs(1) - 1)
    def _():
        sfin_ref[0] = s_scr[...]


def _ret_tables(chunk, rows_per_seq):
    lg = jnp.log1p(-jnp.exp2(-5.0 - jnp.arange(N_HEADS_RET, dtype=F32)))
    r = jnp.arange(chunk)
    t = (r % rows_per_seq).astype(F32)
    diff = t[:, None] - t[None, :]
    ok = (diff >= 0) & ((r // rows_per_seq)[:, None] == (r // rows_per_seq)[None, :])
    scale = HD_RET ** -0.5
    dm = jnp.where(ok[None], jnp.exp(jnp.where(ok, diff, 0.0)[None] * lg[:, None, None]), 0.0) * scale
    qd = jnp.exp((t + 1.0)[None, :] * lg[:, None])
    kd = jnp.exp((rows_per_seq - 1.0 - t)[None, :] * lg[:, None]) * scale
    cd = jnp.exp(rows_per_seq * lg)
    bc = lambda a: jnp.broadcast_to(a[:, :, None], (N_HEADS_RET, chunk, HD_RET))
    return dm, bc(qd), bc(kd), jnp.broadcast_to(cd[:, None, None], (N_HEADS_RET, HD_RET, HD_RET))


def _ret_prompt(qr, kr, vr, gr, gn_w, batch):
    n = qr.shape[0]
    nc = n // batch // RET_CHUNK
    tabs = _ret_tables(RET_CHUNK, RET_CHUNK)
    blk = pl.BlockSpec((RET_CHUNK, D_RET), lambda b, c: (b * nc + c, 0))
    st_shape = (N_HEADS_RET, HD_RET, HD_RET)
    return pl.pallas_call(
        _ret_prompt_body,
        grid=(batch, nc),
        in_specs=[blk] * 4 + [_const_spec((1, D_RET))] + [_const_spec(t.shape) for t in tabs],
        out_specs=[blk, pl.BlockSpec((1,) + st_shape, lambda b, c: (b, 0, 0, 0))],
        out_shape=[jax.ShapeDtypeStruct((n, D_RET), BF16),
                   jax.ShapeDtypeStruct((batch,) + st_shape, F32)],
        scratch_shapes=[pltpu.VMEM(st_shape, F32)],
        compiler_params=_cparams(2),
        name="ret_prompt",
    )(qr, kr, vr, gr, gn_w, *tabs)


def _ret_sample_body(q_ref, k_ref, v_ref, g_ref, gn_ref, dm_ref, qd_ref, kd_ref, cd_ref, s0_ref,
                     y_ref, s1_ref, *, rows_per_seq):
    rows = q_ref.shape[0]
    seqs = rows // rows_per_seq
    seq_of_row = lax.broadcasted_iota(jnp.int32, (rows, HD_RET), 0) // rows_per_seq
    for h in range(N_HEADS_RET):
        sl = slice(h * HD_RET, (h + 1) * HD_RET)
        q, k, v = q_ref[:, sl], k_ref[:, sl], v_ref[:, sl]
        qb, vb = q.astype(BF16), v.astype(BF16)
        s = _dot_nt(qb, k.astype(BF16)) * dm_ref[h]
        inner = _dot(s.astype(BF16), vb)
        kd = k * kd_ref[h]
        cross = jnp.zeros_like(inner)
        for b in range(seqs):
            mine = seq_of_row == b
            state = s0_ref[b, h]
            cross = jnp.where(mine, _dot(qb, state.astype(BF16)), cross)
            kd_b = jnp.where(mine, kd, 0.0).astype(BF16)
            s1_ref[b, h] = cd_ref[h] * state + _dot_tn(kd_b, vb)
        y_ref[:, sl] = _gn_gate(inner + cross * qd_ref[h], g_ref[:, sl], gn_ref[:, sl]).astype(y_ref.dtype)


def _ret_sample(qr, kr, vr, gr, gn_w, s0, rows_per_seq):
    n = qr.shape[0]
    rows = 16
    seqs = rows // rows_per_seq
    tabs = _ret_tables(rows, rows_per_seq)
    blk = pl.BlockSpec((rows, D_RET), lambda i: (i, 0))
    st_blk = pl.BlockSpec((seqs, N_HEADS_RET, HD_RET, HD_RET), lambda i: (i, 0, 0, 0))
    return pl.pallas_call(
        functools.partial(_ret_sample_body, rows_per_seq=rows_per_seq),
        grid=(n // rows,),
        in_specs=[blk] * 4 + [_const_spec((1, D_RET))] + [_const_spec(t.shape) for t in tabs] + [st_blk],
        out_specs=[blk, st_blk],
        out_shape=[jax.ShapeDtypeStruct((n, D_RET), BF16), jax.ShapeDtypeStruct(s0.shape, F32)],
        compiler_params=_cparams(1),
        name="ret_sample",
    )(qr, kr, vr, gr, gn_w, *tabs, s0)


def _attn_prompt_body(q_ref, kc_ref, kp_ref, vc_ref, vp_ref, o_ref, l_ref, *, n_blocks, n_res):
    first_group = pl.program_id(2) == 0
    row = lax.broadcasted_iota(jnp.int32, (ATT_BLOCK, ATT_BLOCK), 0)
    col = lax.broadcasted_iota(jnp.int32, (ATT_BLOCK, ATT_BLOCK), 1)
    tri = col <= row
    lane_lo = lax.broadcasted_iota(jnp.int32, (ATT_BLOCK, LANES), 1) < HD_ATT
    neg = -jnp.inf
    for j in range(n_blocks):
        rs = slice(j * ATT_BLOCK, (j + 1) * ATT_BLOCK)
        for g in range(n_res):
            for pr in range(D_ATT // LANES):
                cs = slice(g * D_ATT + pr * LANES, g * D_ATT + (pr + 1) * LANES)
                q, kc, vc = q_ref[0, rs, cs], kc_ref[0, rs, cs], vc_ref[0, rs, cs]
                if j == 0:
                    kp, vp = kp_ref[0, :, cs], vp_ref[0, :, cs]
                else:
                    ps = slice((j - 1) * ATT_BLOCK, j * ATT_BLOCK)
                    kp, vp = kc_ref[0, ps, cs], vc_ref[0, ps, cs]
                kk = jnp.concatenate([kp, kc], axis=0)
                vv = jnp.concatenate([vp, vc], axis=0)
                kp32, vp32 = kp.astype(F32), vp.astype(F32)
                o_pair = l_pair = None
                for hh in range(2):
                    mine = lane_lo if hh == 0 else jnp.logical_not(lane_lo)
                    qm = jnp.where(mine, q, jnp.zeros_like(q))
                    s_full = _dot_nt(qm, kk)
                    sp, sc = s_full[:, :ATT_BLOCK], s_full[:, ATT_BLOCK:]
                    dg = jnp.sum(qm.astype(F32) * kp32, axis=-1, keepdims=True)
                    if j == 0:
                        sp = jnp.where(first_group, neg, sp)
                        dg = jnp.where(first_group, neg, dg)
                    s = jnp.where(tri, sc, sp)
                    m = jnp.maximum(jnp.max(s, axis=-1, keepdims=True), dg)
                    p = jnp.exp(s - m)
                    pd = jnp.exp(dg - m)
                    den = jnp.sum(p, axis=-1, keepdims=True) + pd
                    pcat = jnp.concatenate([jnp.where(tri, 0.0, p), jnp.where(tri, p, 0.0)], axis=1)
                    o = (_dot(pcat.astype(BF16), vv) + pd * vp32) * (1.0 / den)
                    lse = jnp.broadcast_to(m + jnp.log(den), o.shape)
                    o_pair = o if hh == 0 else jnp.where(mine, o, o_pair)
                    l_pair = lse if hh == 0 else jnp.where(mine, lse, l_pair)
                o_ref[0, rs, cs] = o_pair.astype(o_ref.dtype)
                l_ref[0, rs, cs] = l_pair


def _attn_prompt_pattern(qa, ka, va, batch, dil):
    n = qa.shape[0]
    seq = n // batch
    rows = seq // dil
    n_res = min(dil, 4)
    n_blocks = 4 // n_res
    view = lambda a: a.reshape(batch, rows, dil * D_ATT)
    grid = (batch, dil // n_res, rows // (n_blocks * ATT_BLOCK))
    cur = pl.BlockSpec((1, n_blocks * ATT_BLOCK, n_res * D_ATT), lambda b, r, i: (b, i, r))
    prev = pl.BlockSpec((1, ATT_BLOCK, n_res * D_ATT),
                        lambda b, r, i: (b, jnp.maximum(i * n_blocks - 1, 0), r))
    o, lse = pl.pallas_call(
        functools.partial(_attn_prompt_body, n_blocks=n_blocks, n_res=n_res),
        grid=grid,
        in_specs=[cur, cur, prev, cur, prev],
        out_specs=[cur, cur],
        out_shape=[jax.ShapeDtypeStruct((batch, rows, dil * D_ATT), BF16),
                   jax.ShapeDtypeStruct((batch, rows, dil * D_ATT), F32)],
        compiler_params=_cparams(3),
        name=f"attn_prompt_d{dil}",
    )(view(qa), view(ka), view(ka), view(va), view(va))
    return o.reshape(n, D_ATT), lse.reshape(n, D_ATT)


def _attn_sample_body(q_ref, kn_ref, vn_ref, ck_ref, cv_ref, o_ref, *, n_new):
    b = pl.program_id(0)
    n_ctx = ck_ref.shape[1]
    rows = n_new * N_HEADS_ATT
    head_of_lane = lax.broadcasted_iota(jnp.int32, (N_HEADS_ATT, D_ATT), 1) // HD_ATT
    head_mask = head_of_lane == lax.broadcasted_iota(jnp.int32, (N_HEADS_ATT, D_ATT), 0)

    def spread(ref, t):
        r = ref[pl.ds(b * n_new + t, 1), :].astype(F32)
        return jnp.where(head_mask, jnp.broadcast_to(r, head_mask.shape), 0.0)

    qbd = jnp.concatenate([spread(q_ref, t) for t in range(n_new)], axis=0)
    kc = ck_ref[0].astype(BF16)
    vc = cv_ref[0].astype(BF16)
    s = _dot_nt(qbd.astype(BF16), kc)

    def multiplicity(delta):
        c = jnp.zeros(delta.shape, F32)
        for window, dil in DIL_PATTERNS:
            c = c + jnp.where((delta >= 0) & (delta % dil == 0) & (delta <= window), 1.0, 0.0)
        return c

    t_row = lax.broadcasted_iota(jnp.int32, (rows, n_ctx), 0) // N_HEADS_ATT
    key = lax.broadcasted_iota(jnp.int32, (rows, n_ctx), 1)
    cnt = multiplicity(n_ctx + t_row - key)
    t_col = lax.broadcasted_iota(jnp.int32, (rows, 1), 0) // N_HEADS_ATT
    kn = [jnp.broadcast_to(kn_ref[pl.ds(b * n_new + u, 1), :].astype(F32), (rows, D_ATT)) for u in range(n_new)]
    vn = [jnp.broadcast_to(vn_ref[pl.ds(b * n_new + u, 1), :].astype(F32), (rows, D_ATT)) for u in range(n_new)]
    qr = qbd.astype(BF16).astype(F32)
    s_new = [jnp.sum(qr * kn[u].astype(BF16).astype(F32), axis=-1, keepdims=True) for u in range(n_new)]
    cnt_new = [multiplicity(t_col - u) for u in range(n_new)]

    neg = -jnp.inf
    m = jnp.max(jnp.where(cnt > 0, s, neg), axis=-1, keepdims=True)
    for u in range(n_new):
        m = jnp.maximum(m, jnp.where(cnt_new[u] > 0, s_new[u], neg))
    p = cnt * jnp.exp(jnp.where(cnt > 0, s, neg) - m)
    den = jnp.sum(p, axis=-1, keepdims=True)
    o = _dot(p.astype(BF16), vc)
    for u in range(n_new):
        pu = cnt_new[u] * jnp.exp(jnp.where(cnt_new[u] > 0, s_new[u], neg) - m)
        den = den + pu
        o = o + pu * vn[u].astype(BF16).astype(F32)
    o = o * (1.0 / den)
    for t in range(n_new):
        ot = jnp.where(head_mask, o[t * N_HEADS_ATT:(t + 1) * N_HEADS_ATT], 0.0)
        o_ref[pl.ds(b * n_new + t, 1), :] = jnp.sum(ot, axis=0, keepdims=True)


def _attn_sample(qa, ka, va, ck, cv, n_new):
    n = qa.shape[0]
    batch, n_ctx = ck.shape[0], ck.shape[1]
    full = _const_spec((n, D_ATT))
    cache = pl.BlockSpec((1, n_ctx, D_ATT), lambda b: (b, 0, 0))
    return pl.pallas_call(
        functools.partial(_attn_sample_body, n_new=n_new),
        grid=(batch,),
        in_specs=[full, full, full, cache, cache],
        out_specs=pl.BlockSpec((n, D_ATT), lambda b: (0, 0)),
        out_shape=jax.ShapeDtypeStruct((n, D_ATT), F32),
        compiler_params=_cparams(1),
        name="attn_sample",
    )(qa, ka, va, ck, cv)


def _post_body(*refs, n_att, sample, rows_per_seq, tiles_per_seq, final, d_ff, chunk):
    it = iter(refs)
    x_ref, y_ref = next(it), next(it)
    o_refs = [next(it) for _ in range(n_att)]
    l_refs = [next(it) for _ in range(n_att)] if n_att > 1 else []
    wout_ref, n2_ref, wup_ref, cw_ref, cb_ref, wdn_ref = (next(it) for _ in range(6))
    prev1_ref, prev2_ref = (next(it), next(it)) if sample else (None, None)
    fw_ref = next(it) if final else None
    xo_ref, u_ref = next(it), next(it)
    yfin_ref = next(it) if final else None
    carry = None if sample else next(it)

    tm = x_ref.shape[0]
    if n_att > 1:
        ls = [r[...] for r in l_refs]
        m = functools.reduce(jnp.maximum, ls)
        es = [jnp.exp(l - m) for l in ls]
        num = functools.reduce(jnp.add, [e * r[...].astype(F32) for e, r in zip(es, o_refs)])
        o_att = num * (1.0 / functools.reduce(jnp.add, es))
    else:
        o_att = o_refs[0][...]
    mix = _dot(y_ref[...], wout_ref[:D_RET, :]) + _dot(o_att.astype(BF16), wout_ref[D_RET:, :])
    x1 = x_ref[...] + mix
    h2 = _rms(x1, n2_ref[...]).astype(BF16)

    row = lax.broadcasted_iota(jnp.int32, (tm, chunk), 0)
    t = row % rows_per_seq
    if not sample:
        @pl.when(pl.program_id(0) % tiles_per_seq == 0)
        def _():
            carry[...] = jnp.zeros_like(carry)

    def conv(u, cs):
        if sample:
            p1, p2 = prev1_ref[:, cs], prev2_ref[:, cs]
            u_ref[:, cs] = u
        else:
            c6, c7 = carry[6:7, cs], carry[7:8, cs]
            p1 = jnp.broadcast_to(c7, u.shape)
            p2 = jnp.where(row == 0, c6, c7)
            carry[:, cs] = u[tm - 8:, :]
        s1 = jnp.where(t >= 1, pltpu.roll(u, 1, 0), p1)
        s2 = jnp.where(t >= 2, pltpu.roll(u, 2, 0), p2)
        return ((cb_ref[:, cs] + s2 * cw_ref[0:1, cs]) + s1 * cw_ref[1:2, cs]) + u * cw_ref[2:3, cs]

    acc = jnp.zeros((tm, xo_ref.shape[1]), F32)
    for c in range(d_ff // chunk):
        gs = slice(c * chunk, (c + 1) * chunk)
        vs = slice(d_ff + c * chunk, d_ff + (c + 1) * chunk)
        cg = conv(_dot(h2, wup_ref[:, gs]), gs)
        cv = conv(_dot(h2, wup_ref[:, vs]), vs)
        acc = acc + _dot((_silu(cg) * cv).astype(BF16), wdn_ref[gs, :])
    xo = x1 + acc
    xo_ref[...] = xo
    if not sample:
        u_ref[0] = carry[6:8, :]
    if final:
        yfin_ref[...] = _rms(xo, fw_ref[...])


def _post(x, y, o_list, l_list, wout, n2, wup, cw, cb, wdn, *, tm, tiles_per_seq, rows_per_seq,
          prev=None, final_w=None):
    n, d = x.shape
    d_ff = wdn.shape[0]
    sample = prev is not None
    final = final_w is not None
    n_tiles = n // tm
    row = lambda i: (i, 0)
    rblk = lambda w: pl.BlockSpec((tm, w), row)
    ins = [x, y] + list(o_list) + list(l_list) + [wout, n2, wup, cw, cb, wdn]
    specs = ([rblk(d), rblk(D_RET)] + [rblk(D_ATT)] * (len(o_list) + len(l_list))
             + [_const_spec(a.shape) for a in (wout, n2, wup, cw, cb, wdn)])
    if sample:
        ins += list(prev)
        specs += [rblk(2 * d_ff)] * 2
    if final:
        ins.append(final_w)
        specs.append(_const_spec(final_w.shape))
    outs = [jax.ShapeDtypeStruct((n, d), F32)]
    ospecs = [rblk(d)]
    if sample:
        outs.append(jax.ShapeDtypeStruct((n, 2 * d_ff), F32))
        ospecs.append(rblk(2 * d_ff))
        scratch = []
    else:
        outs.append(jax.ShapeDtypeStruct((n_tiles // tiles_per_seq, 2, 2 * d_ff), F32))
        ospecs.append(pl.BlockSpec((1, 2, 2 * d_ff), lambda i: (i // tiles_per_seq, 0, 0)))
        scratch = [pltpu.VMEM((8, 2 * d_ff), F32)]
    if final:
        outs.append(jax.ShapeDtypeStruct((n, d), F32))
        ospecs.append(rblk(d))
    return pl.pallas_call(
        functools.partial(_post_body, n_att=len(o_list), sample=sample, rows_per_seq=rows_per_seq,
                          tiles_per_seq=tiles_per_seq, final=final, d_ff=d_ff, chunk=256),
        grid=(n_tiles,),
        in_specs=specs,
        out_specs=ospecs,
        out_shape=outs,
        scratch_shapes=scratch,
        compiler_params=_cparams(1),
        name="post_sample" if sample else "post_prompt",
    )(*ins)


def _rope_tables(pos):
    def tab(half):
        inv = ROPE_THETA ** (-jnp.arange(half, dtype=F32) / half)
        ang = pos.astype(F32)[:, None] * inv[None, :]
        reps = LANES // (2 * half)
        cos = jnp.tile(jnp.concatenate([jnp.cos(ang)] * 2, axis=-1), (1, reps))
        sin = jnp.tile(jnp.concatenate([-jnp.sin(ang), jnp.sin(ang)], axis=-1), (1, reps))
        return cos, sin
    cr, sr = tab(HD_RET // 2)
    ca, sa = tab(HD_ATT // 2)
    return cr, sr, ca, sa


def kernel(x_prompt, x_sample, cache_win_k, cache_win_v, state_ret, state_conv, norm1_w, w_in, ret_gn_w,
           w_out, norm2_w, w_up, conv_w, conv_b, w_down, final_norm_w):
    batch, seq, d = x_prompt.shape
    dec_batch, dec_seq, _ = x_sample.shape
    depth = w_in.shape[0]
    n_ctx = cache_win_k.shape[2]
    d_ff = w_down.shape[1]
    w_keep = min(max(w for w, _ in DIL_PATTERNS), seq)
    tm = 512
    tiles_per_seq = seq // tm
    n_s = dec_batch * dec_seq

    tabs_p = _rope_tables(jnp.arange(seq, dtype=jnp.int32))
    tabs_s = _rope_tables(PAST_LEN + (jnp.arange(n_s, dtype=jnp.int32) % dec_seq))
    row2 = lambda a: a.reshape(1, -1)
    final_w = row2(final_norm_w)

    xp = x_prompt.reshape(batch * seq, d)
    xs = x_sample.reshape(n_s, d)
    pk, pv, pr, pc, sk, sv, sr, sc = ([] for _ in range(8))
    y_prompt = y_sample = None
    for l in range(depth):
        last = l == depth - 1
        w_in_l, w_out_l = w_in[l].astype(BF16), w_out[l].astype(BF16)
        w_up_l, w_dn_l = w_up[l].astype(BF16), w_down[l].astype(BF16)
        n1, n2, gn, cb = row2(norm1_w[l]), row2(norm2_w[l]), row2(ret_gn_w[l]), row2(conv_b[l])
        ffn = (w_out_l, n2, w_up_l, conv_w[l], cb, w_dn_l)

        qr, kr, vr, gr, qa, ka, va, wk, wv = _in_proj(
            xp, n1, w_in_l, tabs_p, tm=tm, tiles_per_seq=tiles_per_seq, win_tiles=w_keep // tm, act_dtype=BF16)
        y_ret, s_fin = _ret_prompt(qr, kr, vr, gr, gn, batch)
        att = [_attn_prompt_pattern(qa, ka, va, batch, dil) for _, dil in DIL_PATTERNS]
        res = _post(xp, y_ret, [o for o, _ in att], [s for _, s in att], *ffn, tm=tm,
                    tiles_per_seq=tiles_per_seq, rows_per_seq=tm, final_w=final_w if last else None)
        xp, p_conv = res[0], res[1]
        if last:
            y_prompt = res[2]
        pk.append(wk.reshape(batch, w_keep, N_HEADS_ATT, HD_ATT))
        pv.append(wv.reshape(batch, w_keep, N_HEADS_ATT, HD_ATT))
        pr.append(s_fin)
        pc.append(p_conv)

        qr, kr, vr, gr, qa, ka, va, wk, wv = _in_proj(
            xs, n1, w_in_l, tabs_s, tm=n_s, tiles_per_seq=1, win_tiles=1, act_dtype=F32)
        y_ret, s_new = _ret_sample(qr, kr, vr, gr, gn, state_ret[l], dec_seq)
        o_att = _attn_sample(qa, ka, va, cache_win_k[l].reshape(dec_batch, n_ctx, D_ATT),
                             cache_win_v[l].reshape(dec_batch, n_ctx, D_ATT), dec_seq)
        ctx = state_conv[l]
        zero = jnp.zeros((dec_batch, dec_seq - 1, 2 * d_ff), F32)
        prev1 = jnp.concatenate([ctx[:, 1:2], zero], axis=1).reshape(n_s, 2 * d_ff)
        prev2 = jnp.concatenate([ctx, zero[:, 1:]], axis=1).reshape(n_s, 2 * d_ff)
        res = _post(xs, y_ret, [o_att], [], *ffn, tm=n_s, tiles_per_seq=1, rows_per_seq=dec_seq,
                    prev=(prev1, prev2), final_w=final_w if last else None)
        xs, u_s = res[0], res[1]
        if last:
            y_sample = res[2]
        sk.append(wk.reshape(dec_batch, dec_seq, N_HEADS_ATT, HD_ATT))
        sv.append(wv.reshape(dec_batch, dec_seq, N_HEADS_ATT, HD_ATT))
        sr.append(s_new)
        sc.append(u_s.reshape(dec_batch, dec_seq, 2 * d_ff)[:, dec_seq - 2:])

    return (y_prompt.reshape(batch, seq, d), y_sample.reshape(dec_batch, dec_seq, d),
            jnp.stack(pk), jnp.stack(pv), jnp.stack(pr), jnp.stack(pc),
            jnp.stack(sk), jnp.stack(sv), jnp.stack(sr), jnp.stack(sc))
```

```python
import functools

import jax
import jax.numpy as jnp
from jax import lax
from jax.experimental import pallas as pl
from jax.experimental.pallas import tpu as pltpu

F32 = jnp.float32
BF16 = jnp.bfloat16

PAST_LEN = 8192
N_HEADS_RET = 4
HD_RET = 128
N_HEADS_ATT = 8
HD_ATT = 64
D_RET = N_HEADS_RET * HD_RET
D_ATT = N_HEADS_ATT * HD_ATT
DIL_PATTERNS = ((128, 1), (512, 4), (2048, 16))
RET_CHUNK = 128
ROPE_THETA = 10000.0
EPS = 1e-6

LANES = 128
ATT_BLOCK = 128
ATT_SPAN = 2048
N_PAIRS = D_ATT // LANES
SEC = 512
VMEM_LIMIT = 56 * 1024 * 1024


def _cparams(n_axes):
    return pltpu.CompilerParams(dimension_semantics=("arbitrary",) * n_axes,
                                vmem_limit_bytes=VMEM_LIMIT)


def _const_spec(shape):
    zeros = (0,) * len(shape)
    return pl.BlockSpec(shape, lambda *_: zeros, pipeline_mode=pl.Buffered(1))


def _rms(x, w):
    ms = jnp.mean(x * x, axis=-1, keepdims=True)
    return x * lax.rsqrt(ms + EPS) * w


def _silu(g):
    return g * (1.0 / (1.0 + jnp.exp(-g)))


def _dot(a, b):
    return jnp.dot(a, b, preferred_element_type=F32)


def _dot_nt(a, b):
    return lax.dot_general(a, b, (((1,), (1,)), ((), ())), preferred_element_type=F32)


def _dot_tn(a, b):
    return lax.dot_general(a, b, (((0,), (0,)), ((), ())), preferred_element_type=F32)


def _in_proj_body(*refs, sample, tiles_per_seq, win_tiles, n_aliased):
    x_ref, nw_ref, w_ref, cr_ref, sr_ref, ca_ref, sa_ref = refs[n_aliased:n_aliased + 7]
    rest = refs[n_aliased + 7:]
    qr_ref, kr_ref, vr_ref, gr_ref, qa_ref, ka_ref, va_ref = rest[:7]
    wk_ref, wv_ref = (None, None) if sample else rest[7:9]
    tm = x_ref.shape[0]
    h = _rms(x_ref[...], nw_ref[...]).astype(BF16)
    cr, sr, ca, sa = cr_ref[...], sr_ref[...], ca_ref[...], sa_ref[...]
    lane = lax.broadcasted_iota(jnp.int32, cr.shape, 1)
    first_half = (lane % HD_ATT) < (HD_ATT // 2)

    def sec(s):
        return _dot(h, w_ref[:, s * SEC:(s + 1) * SEC])

    def rot_ret(z):
        return z * cr + pltpu.roll(z, HD_RET // 2, 1) * sr

    def rot_att(z):
        partner = jnp.where(first_half, pltpu.roll(z, LANES - HD_ATT // 2, 1),
                            pltpu.roll(z, HD_ATT // 2, 1))
        return z * ca + partner * sa

    def chunks(s, fn):
        z = sec(s)
        return [fn(z[:, c * LANES:(c + 1) * LANES]) for c in range(SEC // LANES)]

    def emit_wide(s, out_ref, fn):
        for c, v in enumerate(chunks(s, fn)):
            out_ref[:, c * LANES:(c + 1) * LANES] = v.astype(out_ref.dtype)

    def emit_att(s, out_ref, fn, win_ref=None):
        vs = chunks(s, fn)
        if sample:
            out_ref[...] = jnp.concatenate(vs, axis=-1).reshape(tm, N_HEADS_ATT, HD_ATT)
            return
        for c, v in enumerate(vs):
            out_ref[c] = v
        if win_ref is not None:
            @pl.when((pl.program_id(0) % tiles_per_seq) >= (tiles_per_seq - win_tiles))
            def _():
                win_ref[...] = jnp.concatenate(vs, axis=-1).reshape(tm, N_HEADS_ATT, HD_ATT)

    ident = lambda z: z
    emit_wide(0, qr_ref, rot_ret)
    emit_wide(1, kr_ref, rot_ret)
    emit_wide(2, vr_ref, ident)
    emit_wide(3, gr_ref, ident)
    emit_att(4, qa_ref, lambda z: rot_att(z) * (HD_ATT ** -0.5))
    emit_att(5, ka_ref, rot_att, wk_ref)
    emit_att(6, va_ref, ident, wv_ref)


def _in_proj(x, nw, w, tabs, *, tm, tiles_per_seq, win_tiles=0, win_stack=None, layer=0, depth=1):
    n, d = x.shape
    sample = win_stack is None
    aliased = (not sample) and win_stack[0] is not None
    n_tiles = n // tm
    n_seq = n_tiles // tiles_per_seq
    tab_tiles = tabs[0].shape[0] // tm
    row = lambda i: (i, 0)
    tab = lambda i: (i % tab_tiles, 0)
    ins = [x, nw, w, *tabs]
    in_specs = ([pl.BlockSpec((tm, d), row), _const_spec((1, d)), _const_spec(w.shape)]
                + [pl.BlockSpec((tm, LANES), tab)] * 4)
    wide = pl.BlockSpec((tm, SEC), row)
    if sample:
        out_shape = ([jax.ShapeDtypeStruct((n, SEC), F32)] * 4
                     + [jax.ShapeDtypeStruct((n, N_HEADS_ATT, HD_ATT), F32)] * 3)
        out_specs = [wide] * 4 + [pl.BlockSpec((tm, N_HEADS_ATT, HD_ATT), lambda i: (i, 0, 0))] * 3
        aliases = {}
    else:
        first_win = tiles_per_seq - win_tiles
        win = lambda i: (layer, i // tiles_per_seq, jnp.maximum(i % tiles_per_seq - first_win, 0), 0, 0)
        stack = jax.ShapeDtypeStruct((depth, n_seq, win_tiles * tm, N_HEADS_ATT, HD_ATT), F32)
        out_shape = ([jax.ShapeDtypeStruct((n, SEC), BF16)] * 4
                     + [jax.ShapeDtypeStruct((N_PAIRS, n, LANES), F32)] * 3 + [stack] * 2)
        out_specs = ([wide] * 4 + [pl.BlockSpec((N_PAIRS, tm, LANES), lambda i: (0, i, 0))] * 3
                     + [pl.BlockSpec((None, None, tm, N_HEADS_ATT, HD_ATT), win)] * 2)
        aliases = {}
        if aliased:
            ins = list(win_stack) + ins
            in_specs = [pl.BlockSpec(memory_space=pl.ANY)] * 2 + in_specs
            aliases = {0: 7, 1: 8}
    return pl.pallas_call(
        functools.partial(_in_proj_body, sample=sample, tiles_per_seq=tiles_per_seq, win_tiles=win_tiles,
                          n_aliased=len(aliases)),
        grid=(n_tiles,),
        in_specs=in_specs,
        out_specs=out_specs,
        out_shape=out_shape,
        input_output_aliases=aliases,
        compiler_params=_cparams(1),
        name="in_proj_sample" if sample else "in_proj",
    )(*ins)


def _gn_gate(o, g, gn_w):
    mu = jnp.mean(o, axis=-1, keepdims=True)
    d = o - mu
    var = jnp.mean(d * d, axis=-1, keepdims=True)
    y = d * lax.rsqrt(var + EPS) * gn_w
    return _silu(g.astype(F32)) * y


def _ret_prompt_body(q_ref, k_ref, v_ref, g_ref, gn_ref, dm_ref, qd_ref, kd_ref, cd_ref,
                     y_ref, sfin_ref, s_scr):
    step = pl.program_id(1)

    @pl.when(step == 0)
    def _():
        s_scr[...] = jnp.zeros_like(s_scr)

    for h in range(N_HEADS_RET):
        sl = slice(h * HD_RET, (h + 1) * HD_RET)
        state = s_scr[h]
        for c in range(q_ref.shape[0] // RET_CHUNK):
            rs = slice(c * RET_CHUNK, (c + 1) * RET_CHUNK)
            q, k, v = q_ref[rs, sl], k_ref[rs, sl], v_ref[rs, sl]
            s = _dot_nt(q, k) * dm_ref[h]
            inner = _dot(s.astype(BF16), v)
            cross = _dot(q, state.astype(BF16)) * qd_ref[h]
            kd = (k.astype(F32) * kd_ref[h]).astype(BF16)
            state = cd_ref[h] * state + _dot_tn(kd, v)
            y_ref[rs, sl] = _gn_gate(inner + cross, g_ref[rs, sl], gn_ref[:, sl]).astype(y_ref.dtype)
        s_scr[h] = state

    @pl.when(step == pl.num_programs(1) - 1)
    def _():
        sfin_ref[0] = s_scr[...]


def _ret_tables(chunk, rows_per_seq):
    lg = jnp.log1p(-jnp.exp2(-5.0 - jnp.arange(N_HEADS_RET, dtype=F32)))
    r = jnp.arange(chunk)
    t = (r % rows_per_seq).astype(F32)
    diff = t[:, None] - t[None, :]
    ok = (diff >= 0) & ((r // rows_per_seq)[:, None] == (r // rows_per_seq)[None, :])
    scale = HD_RET ** -0.5
    dm = jnp.where(ok[None], jnp.exp(jnp.where(ok, diff, 0.0)[None] * lg[:, None, None]), 0.0) * scale
    qd = jnp.exp((t + 1.0)[None, :] * lg[:, None])
    kd = jnp.exp((rows_per_seq - 1.0 - t)[None, :] * lg[:, None]) * scale
    cd = jnp.exp(rows_per_seq * lg)
    bc = lambda a: jnp.broadcast_to(a[:, :, None], (N_HEADS_RET, chunk, HD_RET))
    return dm, bc(qd), bc(kd), jnp.broadcast_to(cd[:, None, None], (N_HEADS_RET, HD_RET, HD_RET))


def _ret_prompt(qr, kr, vr, gr, gn_w, batch):
    n = qr.shape[0]
    rows = 4 * RET_CHUNK
    steps = n // batch // rows
    tabs = _ret_tables(RET_CHUNK, RET_CHUNK)
    blk = pl.BlockSpec((rows, D_RET), lambda b, c: (b * steps + c, 0))
    st_shape = (N_HEADS_RET, HD_RET, HD_RET)
    return pl.pallas_call(
        _ret_prompt_body,
        grid=(batch, steps),
        in_specs=[blk] * 4 + [_const_spec((1, D_RET))] + [_const_spec(t.shape) for t in tabs],
        out_specs=[blk, pl.BlockSpec((1,) + st_shape, lambda b, c: (b, 0, 0, 0))],
        out_shape=[jax.ShapeDtypeStruct((n, D_RET), BF16),
                   jax.ShapeDtypeStruct((batch,) + st_shape, F32)],
        scratch_shapes=[pltpu.VMEM(st_shape, F32)],
        compiler_params=_cparams(2),
        name="ret_prompt",
    )(qr, kr, vr, gr, gn_w, *tabs)


def _ret_sample_body(q_ref, k_ref, v_ref, g_ref, gn_ref, dm_ref, qd_ref, kd_ref, cd_ref, s0_ref,
                     y_ref, s1_ref, *, rows_per_seq):
    rows = q_ref.shape[0]
    seqs = rows // rows_per_seq
    seq_of_row = lax.broadcasted_iota(jnp.int32, (rows, HD_RET), 0) // rows_per_seq
    for h in range(N_HEADS_RET):
        sl = slice(h * HD_RET, (h + 1) * HD_RET)
        q, k, v = q_ref[:, sl], k_ref[:, sl], v_ref[:, sl]
        qb, vb = q.astype(BF16), v.astype(BF16)
        s = _dot_nt(qb, k.astype(BF16)) * dm_ref[h]
        inner = _dot(s.astype(BF16), vb)
        kd = k * kd_ref[h]
        cross = jnp.zeros_like(inner)
        for b in range(seqs):
            mine = seq_of_row == b
            state = s0_ref[b, h]
            cross = jnp.where(mine, _dot(qb, state.astype(BF16)), cross)
            kd_b = jnp.where(mine, kd, 0.0).astype(BF16)
            s1_ref[b, h] = cd_ref[h] * state + _dot_tn(kd_b, vb)
        y_ref[:, sl] = _gn_gate(inner + cross * qd_ref[h], g_ref[:, sl], gn_ref[:, sl]).astype(y_ref.dtype)


def _ret_sample(qr, kr, vr, gr, gn_w, s0, rows_per_seq):
    n = qr.shape[0]
    rows = 16
    seqs = rows // rows_per_seq
    tabs = _ret_tables(rows, rows_per_seq)
    blk = pl.BlockSpec((rows, D_RET), lambda i: (i, 0))
    st_blk = pl.BlockSpec((seqs, N_HEADS_RET, HD_RET, HD_RET), lambda i: (i, 0, 0, 0))
    return pl.pallas_call(
        functools.partial(_ret_sample_body, rows_per_seq=rows_per_seq),
        grid=(n // rows,),
        in_specs=[blk] * 4 + [_const_spec((1, D_RET))] + [_const_spec(t.shape) for t in tabs] + [st_blk],
        out_specs=[blk, st_blk],
        out_shape=[jax.ShapeDtypeStruct((n, D_RET), BF16), jax.ShapeDtypeStruct(s0.shape, F32)],
        compiler_params=_cparams(1),
        name="ret_sample",
    )(qr, kr, vr, gr, gn_w, *tabs, s0)


def _attn_prompt_body(q_ref, k_ref, v_ref, o_ref, kring, vring, m_s, l_s, acc_s):
    span = pl.program_id(2)
    ring_rows = kring.shape[0]
    cur_base = pl.multiple_of((span % 2) * ATT_SPAN, ATT_SPAN)
    kring[pl.ds(cur_base, ATT_SPAN), :] = k_ref[...]
    vring[pl.ds(cur_base, ATT_SPAN), :] = v_ref[...]

    @pl.when(span == 0)
    def _():
        kring[ATT_SPAN:, :] = jnp.zeros((ATT_SPAN, LANES), F32)
        vring[ATT_SPAN:, :] = jnp.zeros((ATT_SPAN, LANES), F32)

    row = lax.broadcasted_iota(jnp.int32, (ATT_BLOCK, ATT_BLOCK), 0)
    col = lax.broadcasted_iota(jnp.int32, (ATT_BLOCK, ATT_BLOCK), 1)
    tri = col <= row
    lane_lo = lax.broadcasted_iota(jnp.int32, (ATT_BLOCK, LANES), 1) < HD_ATT
    neg = -jnp.inf

    def rows(start, dil):
        if dil == 1:
            return pl.ds(pl.multiple_of(start, ATT_BLOCK), ATT_BLOCK)
        return pl.ds(start, ATT_BLOCK, stride=dil)

    def block(local, dil, no_prev):
        cur = cur_base + local
        prev = (cur - ATT_BLOCK * dil) & (ring_rows - 1)
        q = q_ref[rows(local, dil), :].astype(BF16)
        kc, vc = kring[rows(cur, dil), :], vring[rows(cur, dil), :]
        kp, vp = kring[rows(prev, dil), :], vring[rows(prev, dil), :]
        kk = jnp.concatenate([kp, kc], axis=0).astype(BF16)
        vv = jnp.concatenate([vp, vc], axis=0).astype(BF16)
        kp_r, vp_r = kp.astype(BF16).astype(F32), vp.astype(BF16).astype(F32)
        m_pair = l_pair = o_pair = None
        for hh in range(2):
            mine = lane_lo if hh == 0 else jnp.logical_not(lane_lo)
            qm = jnp.where(mine, q, jnp.zeros_like(q))
            s_full = _dot_nt(qm, kk)
            sp = jnp.where(no_prev, neg, s_full[:, :ATT_BLOCK])
            sc = s_full[:, ATT_BLOCK:]
            dg = jnp.where(no_prev, neg, jnp.sum(qm.astype(F32) * kp_r, axis=-1, keepdims=True))
            s = jnp.where(tri, sc, sp)
            m = jnp.maximum(jnp.max(s, axis=-1, keepdims=True), dg)
            p = jnp.exp(s - m)
            pd = jnp.exp(dg - m)
            den = jnp.sum(p, axis=-1, keepdims=True) + pd
            pcat = jnp.concatenate([jnp.where(tri, 0.0, p), jnp.where(tri, p, 0.0)], axis=1)
            o = _dot(pcat.astype(BF16), vv) + pd * vp_r
            mb, lb = jnp.broadcast_to(m, o.shape), jnp.broadcast_to(den, o.shape)
            m_pair = mb if hh == 0 else jnp.where(mine, mb, m_pair)
            l_pair = lb if hh == 0 else jnp.where(mine, lb, l_pair)
            o_pair = o if hh == 0 else jnp.where(mine, o, o_pair)
        return m_pair, l_pair, o_pair

    def merged(sel, new):
        m2, l2, o2 = new
        m1, l1, o1 = m_s[sel, :], l_s[sel, :], acc_s[sel, :]
        m = jnp.maximum(m1, m2)
        a1, a2 = jnp.exp(m1 - m), jnp.exp(m2 - m)
        return m, a1 * l1 + a2 * l2, a1 * o1 + a2 * o2

    first_span = span == 0
    n_blocks = ATT_SPAN // ATT_BLOCK

    def wide_body(i, carry):
        sel = rows(i, 16)
        m, l, o = block(i, 16, first_span)
        m_s[sel, :], l_s[sel, :], acc_s[sel, :] = m, l, o
        return carry

    def mid_body(i, carry):
        local = (i >> 2) * (ATT_BLOCK * 4) + (i & 3)
        sel = rows(local, 4)
        m, l, o = merged(sel, block(local, 4, jnp.logical_and(first_span, i < 4)))
        m_s[sel, :], l_s[sel, :], acc_s[sel, :] = m, l, o
        return carry

    def near_body(i, carry):
        local = pl.multiple_of(i * ATT_BLOCK, ATT_BLOCK)
        sel = rows(local, 1)
        _, l, o = merged(sel, block(local, 1, jnp.logical_and(first_span, i == 0)))
        o_ref[sel, :] = (o * (1.0 / l)).astype(o_ref.dtype)
        return carry

    lax.fori_loop(0, n_blocks, wide_body, 0, unroll=2)
    lax.fori_loop(0, n_blocks, mid_body, 0, unroll=2)
    lax.fori_loop(0, n_blocks, near_body, 0, unroll=2)


def _attn_prompt(qa, ka, va, batch):
    n = qa.shape[1]
    spans = n // batch // ATT_SPAN
    assert [d for _, d in DIL_PATTERNS] == [1, 4, 16] and all(w == ATT_BLOCK * d for w, d in DIL_PATTERNS)
    slab = pl.BlockSpec((None, ATT_SPAN, LANES), lambda b, p, s: (p, b * spans + s, 0))
    return pl.pallas_call(
        _attn_prompt_body,
        grid=(batch, N_PAIRS, spans),
        in_specs=[slab] * 3,
        out_specs=pl.BlockSpec((ATT_SPAN, LANES), lambda b, p, s: (b * spans + s, p)),
        out_shape=jax.ShapeDtypeStruct((n, D_ATT), BF16),
        scratch_shapes=[pltpu.VMEM((2 * ATT_SPAN, LANES), F32)] * 2 + [pltpu.VMEM((ATT_SPAN, LANES), F32)] * 3,
        compiler_params=_cparams(3),
        name="attn_prompt",
    )(qa, ka, va)


def _attn_sample_body(q_ref, kn_ref, vn_ref, kf_ref, vf_ref, kt_ref, vt_ref, o_ref, *, n_new, n_ctx):
    b = pl.program_id(0)
    widest = max(d for _, d in DIL_PATTERNS)
    tail_rows = kt_ref.shape[0]
    tail_start = n_ctx - tail_rows
    rows = n_new * N_HEADS_ATT
    mat = lambda ref_val: ref_val.reshape(-1, HD_ATT).astype(BF16)
    q = mat(q_ref[pl.ds(b * n_new, n_new)])
    k_new, v_new = mat(kn_ref[pl.ds(b * n_new, n_new)]), mat(vn_ref[pl.ds(b * n_new, n_new)])
    k_far, v_far, k_tail, v_tail = mat(kf_ref[...]), mat(vf_ref[...]), mat(kt_ref[...]), mat(vt_ref[...])

    def weight(cols, pos_of_col):
        r = lax.broadcasted_iota(jnp.int32, (rows, cols), 0)
        c = lax.broadcasted_iota(jnp.int32, (rows, cols), 1)
        pos, valid = pos_of_col(c // N_HEADS_ATT)
        delta = n_ctx + r // N_HEADS_ATT - pos
        w = jnp.zeros((rows, cols), F32)
        for window, dil in DIL_PATTERNS:
            w = w + jnp.where((delta >= 0) & (delta % dil == 0) & (delta <= window), 1.0, 0.0)
        return jnp.where(valid & (c % N_HEADS_ATT == r % N_HEADS_ATT), w, 0.0)

    def far_pos(g):
        pos = (g // n_new) * widest + g % n_new
        return pos, pos < tail_start

    parts = [
        (k_far, v_far, weight(k_far.shape[0], far_pos)),
        (k_tail, v_tail, weight(k_tail.shape[0], lambda g: (tail_start + g, g >= 0))),
        (k_new, v_new, weight(k_new.shape[0], lambda g: (n_ctx + g, g >= 0))),
    ]
    neg = -jnp.inf
    scores = [jnp.where(w > 0, _dot_nt(q, k), neg) for k, _, w in parts]
    m = functools.reduce(jnp.maximum, [jnp.max(s, axis=-1, keepdims=True) for s in scores])
    ps = [w * jnp.exp(s - m) for s, (_, _, w) in zip(scores, parts)]
    den = functools.reduce(jnp.add, [jnp.sum(p, axis=-1, keepdims=True) for p in ps])
    o = functools.reduce(jnp.add, [_dot(p.astype(BF16), v) for p, (_, v, _) in zip(ps, parts)])
    o_ref[pl.ds(b * n_new, n_new)] = (o * (1.0 / den)).reshape(n_new, N_HEADS_ATT, HD_ATT)


def _attn_sample(qa, ka, va, cache_k, cache_v, layer, n_new):
    n = qa.shape[0]
    depth, batch, n_ctx = cache_k.shape[:3]
    widest = max(d for _, d in DIL_PATTERNS)
    tail = max(w for w, d in DIL_PATTERNS if d < widest)
    assert n_new <= widest and n_ctx % widest == 0 and n_ctx % tail == 0
    full = pl.BlockSpec((n, N_HEADS_ATT, HD_ATT), lambda b: (0, 0, 0), pipeline_mode=pl.Buffered(1))
    grouped = lambda c: c.reshape(depth, batch, n_ctx // widest, widest, N_HEADS_ATT, HD_ATT)
    far = pl.BlockSpec((None, None, n_ctx // widest, n_new, N_HEADS_ATT, HD_ATT),
                       lambda b: (layer, b, 0, 0, 0, 0))
    last = pl.BlockSpec((None, None, tail, N_HEADS_ATT, HD_ATT), lambda b: (layer, b, n_ctx // tail - 1, 0, 0))
    return pl.pallas_call(
        functools.partial(_attn_sample_body, n_new=n_new, n_ctx=n_ctx),
        grid=(batch,),
        in_specs=[full, full, full, far, far, last, last],
        out_specs=pl.BlockSpec((n, N_HEADS_ATT, HD_ATT), lambda b: (0, 0, 0)),
        out_shape=jax.ShapeDtypeStruct((n, N_HEADS_ATT, HD_ATT), F32),
        compiler_params=_cparams(1),
        name="attn_sample",
    )(qa, ka, va, grouped(cache_k), grouped(cache_v), cache_k, cache_v)


def _post_body(*refs, sample, rows_per_seq, tiles_per_seq, final, d_ff, chunk):
    it = iter(refs)
    x_ref, y_ref, o_ref = next(it), next(it), next(it)
    wout_ref, n2_ref, wup_ref, cw_ref, cb_ref, wdn_ref = (next(it) for _ in range(6))
    prev1_ref, prev2_ref = (next(it), next(it)) if sample else (None, None)
    fw_ref = next(it) if final else None
    xo_ref, u_ref = next(it), next(it)
    yfin_ref = next(it) if final else None
    carry = None if sample else next(it)

    tm = x_ref.shape[0]
    o_att = o_ref[...].reshape(tm, D_ATT) if sample else o_ref[...]
    mix = _dot(y_ref[...], wout_ref[:D_RET, :]) + _dot(o_att.astype(BF16), wout_ref[D_RET:, :])
    x1 = x_ref[...] + mix
    h2 = _rms(x1, n2_ref[...]).astype(BF16)

    row = lax.broadcasted_iota(jnp.int32, (tm, chunk), 0)
    t = row % rows_per_seq
    if not sample:
        @pl.when(pl.program_id(0) % tiles_per_seq == 0)
        def _():
            carry[...] = jnp.zeros_like(carry)

    def conv(u, cs):
        if sample:
            p1, p2 = prev1_ref[:, cs], prev2_ref[:, cs]
            u_ref[:, cs] = u
        else:
            c6, c7 = carry[6:7, cs], carry[7:8, cs]
            p1 = jnp.broadcast_to(c7, u.shape)
            p2 = jnp.where(row == 0, c6, c7)
            carry[:, cs] = u[tm - 8:, :]
        s1 = jnp.where(t >= 1, pltpu.roll(u, 1, 0), p1)
        s2 = jnp.where(t >= 2, pltpu.roll(u, 2, 0), p2)
        return ((cb_ref[:, cs] + s2 * cw_ref[0:1, cs]) + s1 * cw_ref[1:2, cs]) + u * cw_ref[2:3, cs]

    acc = jnp.zeros((tm, xo_ref.shape[1]), F32)
    for c in range(d_ff // chunk):
        gs = slice(c * chunk, (c + 1) * chunk)
        vs = slice(d_ff + c * chunk, d_ff + (c + 1) * chunk)
        cg = conv(_dot(h2, wup_ref[:, gs]), gs)
        cv = conv(_dot(h2, wup_ref[:, vs]), vs)
        acc = acc + _dot((_silu(cg) * cv).astype(BF16), wdn_ref[gs, :])
    xo = x1 + acc
    xo_ref[...] = xo
    if not sample:
        u_ref[0] = carry[6:8, :]
    if final:
        yfin_ref[...] = _rms(xo, fw_ref[...])


def _post(x, y, o_att, wout, n2, wup, cw, cb, wdn, *, tm, tiles_per_seq, rows_per_seq,
          prev=None, final_w=None):
    n, d = x.shape
    d_ff = wdn.shape[0]
    sample = prev is not None
    final = final_w is not None
    n_tiles = n // tm
    row = lambda i: (i, 0)
    rblk = lambda w: pl.BlockSpec((tm, w), row)
    o_spec = pl.BlockSpec((tm, N_HEADS_ATT, HD_ATT), lambda i: (i, 0, 0)) if sample else rblk(D_ATT)
    ins = [x, y, o_att, wout, n2, wup, cw, cb, wdn]
    specs = [rblk(d), rblk(D_RET), o_spec] + [_const_spec(a.shape) for a in (wout, n2, wup, cw, cb, wdn)]
    if sample:
        ins += list(prev)
        specs += [rblk(2 * d_ff)] * 2
    if final:
        ins.append(final_w)
        specs.append(_const_spec(final_w.shape))
    outs = [jax.ShapeDtypeStruct((n, d), F32)]
    ospecs = [rblk(d)]
    if sample:
        outs.append(jax.ShapeDtypeStruct((n, 2 * d_ff), F32))
        ospecs.append(rblk(2 * d_ff))
        scratch = []
    else:
        outs.append(jax.ShapeDtypeStruct((n_tiles // tiles_per_seq, 2, 2 * d_ff), F32))
        ospecs.append(pl.BlockSpec((1, 2, 2 * d_ff), lambda i: (i // tiles_per_seq, 0, 0)))
        scratch = [pltpu.VMEM((8, 2 * d_ff), F32)]
    if final:
        outs.append(jax.ShapeDtypeStruct((n, d), F32))
        ospecs.append(rblk(d))
    return pl.pallas_call(
        functools.partial(_post_body, sample=sample, rows_per_seq=rows_per_seq,
                          tiles_per_seq=tiles_per_seq, final=final, d_ff=d_ff, chunk=256),
        grid=(n_tiles,),
        in_specs=specs,
        out_specs=ospecs,
        out_shape=outs,
        scratch_shapes=scratch,
        compiler_params=_cparams(1),
        name="post_sample" if sample else "post_prompt",
    )(*ins)


def _rope_tables(pos):
    def tab(half):
        inv = ROPE_THETA ** (-jnp.arange(half, dtype=F32) / half)
        ang = pos.astype(F32)[:, None] * inv[None, :]
        reps = LANES // (2 * half)
        cos = jnp.tile(jnp.concatenate([jnp.cos(ang)] * 2, axis=-1), (1, reps))
        sin = jnp.tile(jnp.concatenate([-jnp.sin(ang), jnp.sin(ang)], axis=-1), (1, reps))
        return cos, sin
    cr, sr = tab(HD_RET // 2)
    ca, sa = tab(HD_ATT // 2)
    return cr, sr, ca, sa


def kernel(x_prompt, x_sample, cache_win_k, cache_win_v, state_ret, state_conv, norm1_w, w_in, ret_gn_w,
           w_out, norm2_w, w_up, conv_w, conv_b, w_down, final_norm_w):
    batch, seq, d = x_prompt.shape
    dec_batch, dec_seq, _ = x_sample.shape
    depth = w_in.shape[0]
    d_ff = w_down.shape[1]
    w_keep = min(max(w for w, _ in DIL_PATTERNS), seq)
    tm = 512
    tiles_per_seq = seq // tm
    n_s = dec_batch * dec_seq

    tabs_p = _rope_tables(jnp.arange(seq, dtype=jnp.int32))
    tabs_s = _rope_tables(PAST_LEN + (jnp.arange(n_s, dtype=jnp.int32) % dec_seq))
    row2 = lambda a: a.reshape(1, -1)
    final_w = row2(final_norm_w)

    xp = x_prompt.reshape(batch * seq, d)
    xs = x_sample.reshape(n_s, d)
    win_stack = (None, None)
    pr, pc, sk, sv, sr, sc = ([] for _ in range(6))
    y_prompt = y_sample = None
    for l in range(depth):
        last = l == depth - 1
        w_in_l, w_out_l = w_in[l].astype(BF16), w_out[l].astype(BF16)
        w_up_l, w_dn_l = w_up[l].astype(BF16), w_down[l].astype(BF16)
        n1, n2, gn, cb = row2(norm1_w[l]), row2(norm2_w[l]), row2(ret_gn_w[l]), row2(conv_b[l])
        ffn = (w_out_l, n2, w_up_l, conv_w[l], cb, w_dn_l)

        qr, kr, vr, gr, qa, ka, va, wk, wv = _in_proj(
            xp, n1, w_in_l, tabs_p, tm=tm, tiles_per_seq=tiles_per_seq, win_tiles=w_keep // tm,
            win_stack=win_stack, layer=l, depth=depth)
        win_stack = (wk, wv)
        y_ret, s_fin = _ret_prompt(qr, kr, vr, gr, gn, batch)
        o_att = _attn_prompt(qa, ka, va, batch)
        res = _post(xp, y_ret, o_att, *ffn, tm=tm, tiles_per_seq=tiles_per_seq, rows_per_seq=tm,
                    final_w=final_w if last else None)
        xp, p_conv = res[0], res[1]
        if last:
            y_prompt = res[2]
        pr.append(s_fin)
        pc.append(p_conv)

        qr, kr, vr, gr, qa, ka, va = _in_proj(xs, n1, w_in_l, tabs_s, tm=n_s, tiles_per_seq=1)
        y_ret, s_new = _ret_sample(qr, kr, vr, gr, gn, state_ret[l], dec_seq)
        o_att = _attn_sample(qa, ka, va, cache_win_k, cache_win_v, l, dec_seq)
        ctx = state_conv[l]
        zero = jnp.zeros((dec_batch, dec_seq - 1, 2 * d_ff), F32)
        prev1 = jnp.concatenate([ctx[:, 1:2], zero], axis=1).reshape(n_s, 2 * d_ff)
        prev2 = jnp.concatenate([ctx, zero[:, 1:]], axis=1).reshape(n_s, 2 * d_ff)
        res = _post(xs, y_ret, o_att, *ffn, tm=n_s, tiles_per_seq=1, rows_per_seq=dec_seq,
                    prev=(prev1, prev2), final_w=final_w if last else None)
        xs, u_s = res[0], res[1]
        if last:
            y_sample = res[2]
        sk.append(ka.reshape(dec_batch, dec_seq, N_HEADS_ATT, HD_ATT))
        sv.append(va.reshape(dec_batch, dec_seq, N_HEADS_ATT, HD_ATT))
        sr.append(s_new)
        sc.append(u_s.reshape(dec_batch, dec_seq, 2 * d_ff)[:, dec_seq - 2:])

    return (y_prompt.reshape(batch, seq, d), y_sample.reshape(dec_batch, dec_seq, d),
            win_stack[0], win_stack[1], jnp.stack(pr), jnp.stack(pc),
            jnp.stack(sk), jnp.stack(sv), jnp.stack(sr), jnp.stack(sc))
```

```python
import functools

import jax
import jax.numpy as jnp
from jax import lax
from jax.experimental import pallas as pl
from jax.experimental.pallas import tpu as pltpu

F32 = jnp.float32
BF16 = jnp.bfloat16

PAST_LEN = 8192
N_HEADS_RET = 4
HD_RET = 128
N_HEADS_ATT = 8
HD_ATT = 64
D_RET = N_HEADS_RET * HD_RET
D_ATT = N_HEADS_ATT * HD_ATT
DIL_PATTERNS = ((128, 1), (512, 4), (2048, 16))
RET_CHUNK = 128
ROPE_THETA = 10000.0
EPS = 1e-6

LANES = 128
ATT_BLOCK = 128
ATT_SPAN = 2048
N_PAIRS = D_ATT // LANES
SEC = 512
VMEM_LIMIT = 56 * 1024 * 1024


def _cparams(n_axes):
    return pltpu.CompilerParams(dimension_semantics=("arbitrary",) * n_axes,
                                vmem_limit_bytes=VMEM_LIMIT)


def _const_spec(shape):
    zeros = (0,) * len(shape)
    return pl.BlockSpec(shape, lambda *_: zeros, pipeline_mode=pl.Buffered(1))


def _rms(x, w):
    ms = jnp.mean(x * x, axis=-1, keepdims=True)
    return x * lax.rsqrt(ms + EPS) * w


def _silu(g):
    return g * (1.0 / (1.0 + jnp.exp(-g)))


def _dot(a, b):
    return jnp.dot(a, b, preferred_element_type=F32)


def _dot_nt(a, b):
    return lax.dot_general(a, b, (((1,), (1,)), ((), ())), preferred_element_type=F32)


def _dot_tn(a, b):
    return lax.dot_general(a, b, (((0,), (0,)), ((), ())), preferred_element_type=F32)


def _in_proj_body(*refs, sample, tiles_per_seq, win_tiles, n_aliased):
    x_ref, nw_ref, w_ref, cr_ref, sr_ref, ca_ref, sa_ref = refs[n_aliased:n_aliased + 7]
    rest = refs[n_aliased + 7:]
    qr_ref, kr_ref, vr_ref, gr_ref, qa_ref, ka_ref, va_ref = rest[:7]
    wk_ref, wv_ref = (None, None) if sample else rest[7:9]
    tm = x_ref.shape[0]
    h = _rms(x_ref[...], nw_ref[...]).astype(BF16)
    cr, sr, ca, sa = cr_ref[...], sr_ref[...], ca_ref[...], sa_ref[...]
    lane = lax.broadcasted_iota(jnp.int32, cr.shape, 1)
    first_half = (lane % HD_ATT) < (HD_ATT // 2)

    def sec(s):
        return _dot(h, w_ref[:, s * SEC:(s + 1) * SEC])

    def rot_ret(z):
        return z * cr + pltpu.roll(z, HD_RET // 2, 1) * sr

    def rot_att(z):
        partner = jnp.where(first_half, pltpu.roll(z, LANES - HD_ATT // 2, 1),
                            pltpu.roll(z, HD_ATT // 2, 1))
        return z * ca + partner * sa

    def chunks(s, fn):
        z = sec(s)
        return [fn(z[:, c * LANES:(c + 1) * LANES]) for c in range(SEC // LANES)]

    def emit_wide(s, out_ref, fn):
        for c, v in enumerate(chunks(s, fn)):
            out_ref[:, c * LANES:(c + 1) * LANES] = v.astype(out_ref.dtype)

    def emit_att(s, out_ref, fn, win_ref=None):
        vs = chunks(s, fn)
        if sample:
            out_ref[...] = jnp.concatenate(vs, axis=-1).reshape(tm, N_HEADS_ATT, HD_ATT)
            return
        for c, v in enumerate(vs):
            out_ref[c] = v
        if win_ref is not None:
            @pl.when((pl.program_id(0) % tiles_per_seq) >= (tiles_per_seq - win_tiles))
            def _():
                win_ref[...] = jnp.concatenate(vs, axis=-1).reshape(tm, N_HEADS_ATT, HD_ATT)

    ident = lambda z: z
    emit_wide(0, qr_ref, rot_ret)
    emit_wide(1, kr_ref, rot_ret)
    emit_wide(2, vr_ref, ident)
    emit_wide(3, gr_ref, ident)
    emit_att(4, qa_ref, lambda z: rot_att(z) * (HD_ATT ** -0.5))
    emit_att(5, ka_ref, rot_att, wk_ref)
    emit_att(6, va_ref, ident, wv_ref)


def _in_proj(x, nw, w, tabs, *, tm, tiles_per_seq, win_tiles=0, win_stack=None, layer=0, depth=1):
    n, d = x.shape
    sample = win_stack is None
    aliased = (not sample) and win_stack[0] is not None
    n_tiles = n // tm
    n_seq = n_tiles // tiles_per_seq
    tab_tiles = tabs[0].shape[0] // tm
    row = lambda i: (i, 0)
    tab = lambda i: (i % tab_tiles, 0)
    ins = [x, nw, w, *tabs]
    in_specs = ([pl.BlockSpec((tm, d), row), _const_spec((1, d)), _const_spec(w.shape)]
                + [pl.BlockSpec((tm, LANES), tab)] * 4)
    wide = pl.BlockSpec((tm, SEC), row)
    if sample:
        out_shape = ([jax.ShapeDtypeStruct((n, SEC), F32)] * 4
                     + [jax.ShapeDtypeStruct((n, N_HEADS_ATT, HD_ATT), F32)] * 3)
        out_specs = [wide] * 4 + [pl.BlockSpec((tm, N_HEADS_ATT, HD_ATT), lambda i: (i, 0, 0))] * 3
        aliases = {}
    else:
        first_win = tiles_per_seq - win_tiles
        win = lambda i: (layer, i // tiles_per_seq, jnp.maximum(i % tiles_per_seq - first_win, 0), 0, 0)
        stack = jax.ShapeDtypeStruct((depth, n_seq, win_tiles * tm, N_HEADS_ATT, HD_ATT), F32)
        out_shape = ([jax.ShapeDtypeStruct((n, SEC), BF16)] * 4
                     + [jax.ShapeDtypeStruct((N_PAIRS, n, LANES), F32)] * 3 + [stack] * 2)
        out_specs = ([wide] * 4 + [pl.BlockSpec((N_PAIRS, tm, LANES), lambda i: (0, i, 0))] * 3
                     + [pl.BlockSpec((None, None, tm, N_HEADS_ATT, HD_ATT), win)] * 2)
        aliases = {}
        if aliased:
            ins = list(win_stack) + ins
            in_specs = [pl.BlockSpec(memory_space=pl.ANY)] * 2 + in_specs
            aliases = {0: 7, 1: 8}
    return pl.pallas_call(
        functools.partial(_in_proj_body, sample=sample, tiles_per_seq=tiles_per_seq, win_tiles=win_tiles,
                          n_aliased=len(aliases)),
        grid=(n_tiles,),
        in_specs=in_specs,
        out_specs=out_specs,
        out_shape=out_shape,
        input_output_aliases=aliases,
        compiler_params=_cparams(1),
        name="in_proj_sample" if sample else "in_proj",
    )(*ins)


def _gn_gate(o, g, gn_w):
    mu = jnp.mean(o, axis=-1, keepdims=True)
    d = o - mu
    var = jnp.mean(d * d, axis=-1, keepdims=True)
    y = d * lax.rsqrt(var + EPS) * gn_w
    return _silu(g.astype(F32)) * y


def _ret_prompt_body(q_ref, k_ref, v_ref, g_ref, gn_ref, dm_ref, qd_ref, kd_ref, cd_ref,
                     y_ref, sfin_ref, s_scr):
    step = pl.program_id(1)

    @pl.when(step == 0)
    def _():
        s_scr[...] = jnp.zeros_like(s_scr)

    for h in range(N_HEADS_RET):
        sl = slice(h * HD_RET, (h + 1) * HD_RET)
        state = s_scr[h]
        for c in range(q_ref.shape[0] // RET_CHUNK):
            rs = slice(c * RET_CHUNK, (c + 1) * RET_CHUNK)
            q, k, v = q_ref[rs, sl], k_ref[rs, sl], v_ref[rs, sl]
            s = _dot_nt(q, k) * dm_ref[h]
            inner = _dot(s.astype(BF16), v)
            cross = _dot(q, state.astype(BF16)) * qd_ref[h]
            kd = (k.astype(F32) * kd_ref[h]).astype(BF16)
            state = cd_ref[h] * state + _dot_tn(kd, v)
            y_ref[rs, sl] = _gn_gate(inner + cross, g_ref[rs, sl], gn_ref[:, sl]).astype(y_ref.dtype)
        s_scr[h] = state

    @pl.when(step == pl.num_programs(1) - 1)
    def _():
        sfin_ref[0] = s_scr[...]


def _ret_tables(chunk, rows_per_seq):
    lg = jnp.log1p(-jnp.exp2(-5.0 - jnp.arange(N_HEADS_RET, dtype=F32)))
    r = jnp.arange(chunk)
    t = (r % rows_per_seq).astype(F32)
    diff = t[:, None] - t[None, :]
    ok = (diff >= 0) & ((r // rows_per_seq)[:, None] == (r // rows_per_seq)[None, :])
    scale = HD_RET ** -0.5
    dm = jnp.where(ok[None], jnp.exp(jnp.where(ok, diff, 0.0)[None] * lg[:, None, None]), 0.0) * scale
    qd = jnp.exp((t + 1.0)[None, :] * lg[:, None])
    kd = jnp.exp((rows_per_seq - 1.0 - t)[None, :] * lg[:, None]) * scale
    cd = jnp.exp(rows_per_seq * lg)
    bc = lambda a: jnp.broadcast_to(a[:, :, None], (N_HEADS_RET, chunk, HD_RET))
    return dm, bc(qd), bc(kd), jnp.broadcast_to(cd[:, None, None], (N_HEADS_RET, HD_RET, HD_RET))


def _ret_prompt(qr, kr, vr, gr, gn_w, batch):
    n = qr.shape[0]
    rows = 4 * RET_CHUNK
    steps = n // batch // rows
    tabs = _ret_tables(RET_CHUNK, RET_CHUNK)
    blk = pl.BlockSpec((rows, D_RET), lambda b, c: (b * steps + c, 0))
    st_shape = (N_HEADS_RET, HD_RET, HD_RET)
    return pl.pallas_call(
        _ret_prompt_body,
        grid=(batch, steps),
        in_specs=[blk] * 4 + [_const_spec((1, D_RET))] + [_const_spec(t.shape) for t in tabs],
        out_specs=[blk, pl.BlockSpec((1,) + st_shape, lambda b, c: (b, 0, 0, 0))],
        out_shape=[jax.ShapeDtypeStruct((n, D_RET), BF16),
                   jax.ShapeDtypeStruct((batch,) + st_shape, F32)],
        scratch_shapes=[pltpu.VMEM(st_shape, F32)],
        compiler_params=_cparams(2),
        name="ret_prompt",
    )(qr, kr, vr, gr, gn_w, *tabs)


def _ret_sample_body(q_ref, k_ref, v_ref, g_ref, gn_ref, dm_ref, qd_ref, kd_ref, cd_ref, s0_ref,
                     y_ref, s1_ref, *, rows_per_seq):
    rows = q_ref.shape[0]
    seqs = rows // rows_per_seq
    seq_of_row = lax.broadcasted_iota(jnp.int32, (rows, HD_RET), 0) // rows_per_seq
    for h in range(N_HEADS_RET):
        sl = slice(h * HD_RET, (h + 1) * HD_RET)
        q, k, v = q_ref[:, sl], k_ref[:, sl], v_ref[:, sl]
        qb, vb = q.astype(BF16), v.astype(BF16)
        s = _dot_nt(qb, k.astype(BF16)) * dm_ref[h]
        inner = _dot(s.astype(BF16), vb)
        kd = k * kd_ref[h]
        cross = jnp.zeros_like(inner)
        for b in range(seqs):
            mine = seq_of_row == b
            state = s0_ref[b, h]
            cross = jnp.where(mine, _dot(qb, state.astype(BF16)), cross)
            kd_b = jnp.where(mine, kd, 0.0).astype(BF16)
            s1_ref[b, h] = cd_ref[h] * state + _dot_tn(kd_b, vb)
        y_ref[:, sl] = _gn_gate(inner + cross * qd_ref[h], g_ref[:, sl], gn_ref[:, sl]).astype(y_ref.dtype)


def _ret_sample(qr, kr, vr, gr, gn_w, s0, rows_per_seq):
    n = qr.shape[0]
    rows = 16
    seqs = rows // rows_per_seq
    tabs = _ret_tables(rows, rows_per_seq)
    blk = pl.BlockSpec((rows, D_RET), lambda i: (i, 0))
    st_blk = pl.BlockSpec((seqs, N_HEADS_RET, HD_RET, HD_RET), lambda i: (i, 0, 0, 0))
    return pl.pallas_call(
        functools.partial(_ret_sample_body, rows_per_seq=rows_per_seq),
        grid=(n // rows,),
        in_specs=[blk] * 4 + [_const_spec((1, D_RET))] + [_const_spec(t.shape) for t in tabs] + [st_blk],
        out_specs=[blk, st_blk],
        out_shape=[jax.ShapeDtypeStruct((n, D_RET), BF16), jax.ShapeDtypeStruct(s0.shape, F32)],
        compiler_params=_cparams(1),
        name="ret_sample",
    )(qr, kr, vr, gr, gn_w, *tabs, s0)


def _attn_prompt_body(q_ref, k_ref, v_ref, o_ref, kring, vring, m_s, l_s, acc_s):
    span = pl.program_id(2)
    ring_rows = kring.shape[0]
    cur_base = pl.multiple_of((span % 2) * ATT_SPAN, ATT_SPAN)
    kring[pl.ds(cur_base, ATT_SPAN), :] = k_ref[...]
    vring[pl.ds(cur_base, ATT_SPAN), :] = v_ref[...]

    @pl.when(span == 0)
    def _():
        kring[ATT_SPAN:, :] = jnp.zeros((ATT_SPAN, LANES), F32)
        vring[ATT_SPAN:, :] = jnp.zeros((ATT_SPAN, LANES), F32)

    row = lax.broadcasted_iota(jnp.int32, (ATT_BLOCK, ATT_BLOCK), 0)
    col = lax.broadcasted_iota(jnp.int32, (ATT_BLOCK, ATT_BLOCK), 1)
    tri = col <= row
    lane_lo = lax.broadcasted_iota(jnp.int32, (ATT_BLOCK, LANES), 1) < HD_ATT
    neg = -jnp.inf

    def rows(start, dil):
        if dil == 1:
            return pl.ds(pl.multiple_of(start, ATT_BLOCK), ATT_BLOCK)
        return pl.ds(start, ATT_BLOCK, stride=dil)

    def block(local, dil, no_prev):
        cur = cur_base + local
        prev = (cur - ATT_BLOCK * dil) & (ring_rows - 1)
        q = q_ref[rows(local, dil), :].astype(BF16)
        kc, vc = kring[rows(cur, dil), :], vring[rows(cur, dil), :]
        kp, vp = kring[rows(prev, dil), :], vring[rows(prev, dil), :]
        kk = jnp.concatenate([kp, kc], axis=0).astype(BF16)
        vv = jnp.concatenate([vp, vc], axis=0).astype(BF16)
        kp_r, vp_r = kp.astype(BF16).astype(F32), vp.astype(BF16).astype(F32)
        m_pair = l_pair = o_pair = None
        for hh in range(2):
            mine = lane_lo if hh == 0 else jnp.logical_not(lane_lo)
            qm = jnp.where(mine, q, jnp.zeros_like(q))
            s_full = _dot_nt(qm, kk)
            sp = jnp.where(no_prev, neg, s_full[:, :ATT_BLOCK])
            sc = s_full[:, ATT_BLOCK:]
            dg = jnp.where(no_prev, neg, jnp.sum(qm.astype(F32) * kp_r, axis=-1, keepdims=True))
            s = jnp.where(tri, sc, sp)
            m = jnp.maximum(jnp.max(s, axis=-1, keepdims=True), dg)
            p = jnp.exp(s - m)
            pd = jnp.exp(dg - m)
            den = jnp.sum(p, axis=-1, keepdims=True) + pd
            pcat = jnp.concatenate([jnp.where(tri, 0.0, p), jnp.where(tri, p, 0.0)], axis=1)
            o = _dot(pcat.astype(BF16), vv) + pd * vp_r
            mb, lb = jnp.broadcast_to(m, o.shape), jnp.broadcast_to(den, o.shape)
            m_pair = mb if hh == 0 else jnp.where(mine, mb, m_pair)
            l_pair = lb if hh == 0 else jnp.where(mine, lb, l_pair)
            o_pair = o if hh == 0 else jnp.where(mine, o, o_pair)
        return m_pair, l_pair, o_pair

    def merged(sel, new):
        m2, l2, o2 = new
        m1, l1, o1 = m_s[sel, :], l_s[sel, :], acc_s[sel, :]
        m = jnp.maximum(m1, m2)
        a1, a2 = jnp.exp(m1 - m), jnp.exp(m2 - m)
        return m, a1 * l1 + a2 * l2, a1 * o1 + a2 * o2

    first_span = span == 0
    n_blocks = ATT_SPAN // ATT_BLOCK

    def wide_body(i, carry):
        sel = rows(i, 16)
        m, l, o = block(i, 16, first_span)
        m_s[sel, :], l_s[sel, :], acc_s[sel, :] = m, l, o
        return carry

    def mid_body(i, carry):
        local = (i >> 2) * (ATT_BLOCK * 4) + (i & 3)
        sel = rows(local, 4)
        m, l, o = merged(sel, block(local, 4, jnp.logical_and(first_span, i < 4)))
        m_s[sel, :], l_s[sel, :], acc_s[sel, :] = m, l, o
        return carry

    def near_body(i, carry):
        local = pl.multiple_of(i * ATT_BLOCK, ATT_BLOCK)
        sel = rows(local, 1)
        _, l, o = merged(sel, block(local, 1, jnp.logical_and(first_span, i == 0)))
        o_ref[sel, :] = (o * (1.0 / l)).astype(o_ref.dtype)
        return carry

    lax.fori_loop(0, n_blocks, wide_body, 0, unroll=8)
    lax.fori_loop(0, n_blocks, mid_body, 0, unroll=8)
    lax.fori_loop(0, n_blocks, near_body, 0, unroll=8)


def _attn_prompt(qa, ka, va, batch):
    n = qa.shape[1]
    spans = n // batch // ATT_SPAN
    assert [d for _, d in DIL_PATTERNS] == [1, 4, 16] and all(w == ATT_BLOCK * d for w, d in DIL_PATTERNS)
    slab = pl.BlockSpec((None, ATT_SPAN, LANES), lambda b, p, s: (p, b * spans + s, 0))
    return pl.pallas_call(
        _attn_prompt_body,
        grid=(batch, N_PAIRS, spans),
        in_specs=[slab] * 3,
        out_specs=pl.BlockSpec((ATT_SPAN, LANES), lambda b, p, s: (b * spans + s, p)),
        out_shape=jax.ShapeDtypeStruct((n, D_ATT), BF16),
        scratch_shapes=[pltpu.VMEM((2 * ATT_SPAN, LANES), F32)] * 2 + [pltpu.VMEM((ATT_SPAN, LANES), F32)] * 3,
        compiler_params=_cparams(3),
        name="attn_prompt",
    )(qa, ka, va)


def _attn_sample_body(q_ref, kn_ref, vn_ref, ck_ref, cv_ref, o_ref, *, n_new, n_ctx):
    b = pl.program_id(0)
    widest = max(d for _, d in DIL_PATTERNS)
    n_groups = n_ctx // widest
    tail_rows = max(w for w, d in DIL_PATTERNS if d < widest)
    tail_start = n_ctx - tail_rows
    rows = n_new * N_HEADS_ATT
    mat = lambda val: val.reshape(-1, HD_ATT).astype(BF16)
    far = lambda ref: mat(jnp.concatenate([ref[pl.ds(j, n_groups, stride=widest)] for j in range(n_new)], axis=0))
    q = mat(q_ref[pl.ds(b * n_new, n_new)])
    k_new, v_new = mat(kn_ref[pl.ds(b * n_new, n_new)]), mat(vn_ref[pl.ds(b * n_new, n_new)])
    k_far, v_far = far(ck_ref), far(cv_ref)
    k_tail, v_tail = mat(ck_ref[pl.ds(tail_start, tail_rows)]), mat(cv_ref[pl.ds(tail_start, tail_rows)])

    def weight(cols, pos_of_col):
        r = lax.broadcasted_iota(jnp.int32, (rows, cols), 0)
        c = lax.broadcasted_iota(jnp.int32, (rows, cols), 1)
        pos, valid = pos_of_col(c // N_HEADS_ATT)
        delta = n_ctx + r // N_HEADS_ATT - pos
        w = jnp.zeros((rows, cols), F32)
        for window, dil in DIL_PATTERNS:
            w = w + jnp.where((delta >= 0) & (delta % dil == 0) & (delta <= window), 1.0, 0.0)
        return jnp.where(valid & (c % N_HEADS_ATT == r % N_HEADS_ATT), w, 0.0)

    def far_pos(g):
        pos = (g % n_groups) * widest + g // n_groups
        return pos, pos < tail_start

    parts = [
        (k_far, v_far, weight(k_far.shape[0], far_pos)),
        (k_tail, v_tail, weight(k_tail.shape[0], lambda g: (tail_start + g, g >= 0))),
        (k_new, v_new, weight(k_new.shape[0], lambda g: (n_ctx + g, g >= 0))),
    ]
    neg = -jnp.inf
    scores = [jnp.where(w > 0, _dot_nt(q, k), neg) for k, _, w in parts]
    m = functools.reduce(jnp.maximum, [jnp.max(s, axis=-1, keepdims=True) for s in scores])
    ps = [w * jnp.exp(s - m) for s, (_, _, w) in zip(scores, parts)]
    den = functools.reduce(jnp.add, [jnp.sum(p, axis=-1, keepdims=True) for p in ps])
    o = functools.reduce(jnp.add, [_dot(p.astype(BF16), v) for p, (_, v, _) in zip(ps, parts)])
    o_ref[pl.ds(b * n_new, n_new)] = (o * (1.0 / den)).reshape(n_new, N_HEADS_ATT, HD_ATT)


def _attn_sample(qa, ka, va, cache_k, cache_v, layer, n_new):
    n = qa.shape[0]
    batch, n_ctx = cache_k.shape[1:3]
    widest = max(d for _, d in DIL_PATTERNS)
    assert n_new <= widest and n_ctx % widest == 0
    full = pl.BlockSpec((n, N_HEADS_ATT, HD_ATT), lambda b: (0, 0, 0), pipeline_mode=pl.Buffered(1))
    cache = pl.BlockSpec((None, None, n_ctx, N_HEADS_ATT, HD_ATT), lambda b: (layer, b, 0, 0, 0))
    return pl.pallas_call(
        functools.partial(_attn_sample_body, n_new=n_new, n_ctx=n_ctx),
        grid=(batch,),
        in_specs=[full, full, full, cache, cache],
        out_specs=pl.BlockSpec((n, N_HEADS_ATT, HD_ATT), lambda b: (0, 0, 0)),
        out_shape=jax.ShapeDtypeStruct((n, N_HEADS_ATT, HD_ATT), F32),
        compiler_params=_cparams(1),
        name="attn_sample",
    )(qa, ka, va, cache_k, cache_v)


def _post_body(*refs, sample, rows_per_seq, tiles_per_seq, final, d_ff, chunk):
    it = iter(refs)
    x_ref, y_ref, o_ref = next(it), next(it), next(it)
    wout_ref, n2_ref, wup_ref, cw_ref, cb_ref, wdn_ref = (next(it) for _ in range(6))
    prev1_ref, prev2_ref = (next(it), next(it)) if sample else (None, None)
    fw_ref = next(it) if final else None
    xo_ref, u_ref = next(it), next(it)
    yfin_ref = next(it) if final else None
    carry = None if sample else next(it)

    tm = x_ref.shape[0]
    o_att = o_ref[...].reshape(tm, D_ATT) if sample else o_ref[...]
    mix = _dot(y_ref[...], wout_ref[:D_RET, :]) + _dot(o_att.astype(BF16), wout_ref[D_RET:, :])
    x1 = x_ref[...] + mix
    h2 = _rms(x1, n2_ref[...]).astype(BF16)

    row = lax.broadcasted_iota(jnp.int32, (tm, chunk), 0)
    t = row % rows_per_seq
    if not sample:
        @pl.when(pl.program_id(0) % tiles_per_seq == 0)
        def _():
            carry[...] = jnp.zeros_like(carry)

    def conv(u, cs):
        if sample:
            p1, p2 = prev1_ref[:, cs], prev2_ref[:, cs]
            u_ref[:, cs] = u
        else:
            c6, c7 = carry[6:7, cs], carry[7:8, cs]
            p1 = jnp.broadcast_to(c7, u.shape)
            p2 = jnp.where(row == 0, c6, c7)
            carry[:, cs] = u[tm - 8:, :]
        s1 = jnp.where(t >= 1, pltpu.roll(u, 1, 0), p1)
        s2 = jnp.where(t >= 2, pltpu.roll(u, 2, 0), p2)
        return ((cb_ref[:, cs] + s2 * cw_ref[0:1, cs]) + s1 * cw_ref[1:2, cs]) + u * cw_ref[2:3, cs]

    acc = jnp.zeros((tm, xo_ref.shape[1]), F32)
    for c in range(d_ff // chunk):
        gs = slice(c * chunk, (c + 1) * chunk)
        vs = slice(d_ff + c * chunk, d_ff + (c + 1) * chunk)
        cg = conv(_dot(h2, wup_ref[:, gs]), gs)
        cv = conv(_dot(h2, wup_ref[:, vs]), vs)
        acc = acc + _dot((_silu(cg) * cv).astype(BF16), wdn_ref[gs, :])
    xo = x1 + acc
    xo_ref[...] = xo
    if not sample:
        u_ref[0] = carry[6:8, :]
    if final:
        yfin_ref[...] = _rms(xo, fw_ref[...])


def _post(x, y, o_att, wout, n2, wup, cw, cb, wdn, *, tm, tiles_per_seq, rows_per_seq,
          prev=None, final_w=None):
    n, d = x.shape
    d_ff = wdn.shape[0]
    sample = prev is not None
    final = final_w is not None
    n_tiles = n // tm
    row = lambda i: (i, 0)
    rblk = lambda w: pl.BlockSpec((tm, w), row)
    o_spec = pl.BlockSpec((tm, N_HEADS_ATT, HD_ATT), lambda i: (i, 0, 0)) if sample else rblk(D_ATT)
    ins = [x, y, o_att, wout, n2, wup, cw, cb, wdn]
    specs = [rblk(d), rblk(D_RET), o_spec] + [_const_spec(a.shape) for a in (wout, n2, wup, cw, cb, wdn)]
    if sample:
        ins += list(prev)
        specs += [rblk(2 * d_ff)] * 2
    if final:
        ins.append(final_w)
        specs.append(_const_spec(final_w.shape))
    outs = [jax.ShapeDtypeStruct((n, d), F32)]
    ospecs = [rblk(d)]
    if sample:
        outs.append(jax.ShapeDtypeStruct((n, 2 * d_ff), F32))
        ospecs.append(rblk(2 * d_ff))
        scratch = []
    else:
        outs.append(jax.ShapeDtypeStruct((n_tiles // tiles_per_seq, 2, 2 * d_ff), F32))
        ospecs.append(pl.BlockSpec((1, 2, 2 * d_ff), lambda i: (i // tiles_per_seq, 0, 0)))
        scratch = [pltpu.VMEM((8, 2 * d_ff), F32)]
    if final:
        outs.append(jax.ShapeDtypeStruct((n, d), F32))
        ospecs.append(rblk(d))
    return pl.pallas_call(
        functools.partial(_post_body, sample=sample, rows_per_seq=rows_per_seq,
                          tiles_per_seq=tiles_per_seq, final=final, d_ff=d_ff, chunk=256),
        grid=(n_tiles,),
        in_specs=specs,
        out_specs=ospecs,
        out_shape=outs,
        scratch_shapes=scratch,
        compiler_params=_cparams(1),
        name="post_sample" if sample else "post_prompt",
    )(*ins)


def _rope_tables(pos):
    def tab(half):
        inv = ROPE_THETA ** (-jnp.arange(half, dtype=F32) / half)
        ang = pos.astype(F32)[:, None] * inv[None, :]
        reps = LANES // (2 * half)
        cos = jnp.tile(jnp.concatenate([jnp.cos(ang)] * 2, axis=-1), (1, reps))
        sin = jnp.tile(jnp.concatenate([-jnp.sin(ang), jnp.sin(ang)], axis=-1), (1, reps))
        return cos, sin
    cr, sr = tab(HD_RET // 2)
    ca, sa = tab(HD_ATT // 2)
    return cr, sr, ca, sa


def kernel(x_prompt, x_sample, cache_win_k, cache_win_v, state_ret, state_conv, norm1_w, w_in, ret_gn_w,
           w_out, norm2_w, w_up, conv_w, conv_b, w_down, final_norm_w):
    batch, seq, d = x_prompt.shape
    dec_batch, dec_seq, _ = x_sample.shape
    depth = w_in.shape[0]
    d_ff = w_down.shape[1]
    w_keep = min(max(w for w, _ in DIL_PATTERNS), seq)
    tm = 512
    tiles_per_seq = seq // tm
    n_s = dec_batch * dec_seq

    tabs_p = _rope_tables(jnp.arange(seq, dtype=jnp.int32))
    tabs_s = _rope_tables(PAST_LEN + (jnp.arange(n_s, dtype=jnp.int32) % dec_seq))
    row2 = lambda a: a.reshape(1, -1)
    final_w = row2(final_norm_w)

    xp = x_prompt.reshape(batch * seq, d)
    xs = x_sample.reshape(n_s, d)
    win_stack = (None, None)
    pr, pc, sk, sv, sr, sc = ([] for _ in range(6))
    y_prompt = y_sample = None
    for l in range(depth):
        last = l == depth - 1
        w_in_l, w_out_l = w_in[l].astype(BF16), w_out[l].astype(BF16)
        w_up_l, w_dn_l = w_up[l].astype(BF16), w_down[l].astype(BF16)
        n1, n2, gn, cb = row2(norm1_w[l]), row2(norm2_w[l]), row2(ret_gn_w[l]), row2(conv_b[l])
        ffn = (w_out_l, n2, w_up_l, conv_w[l], cb, w_dn_l)

        qr, kr, vr, gr, qa, ka, va, wk, wv = _in_proj(
            xp, n1, w_in_l, tabs_p, tm=tm, tiles_per_seq=tiles_per_seq, win_tiles=w_keep // tm,
            win_stack=win_stack, layer=l, depth=depth)
        win_stack = (wk, wv)
        y_ret, s_fin = _ret_prompt(qr, kr, vr, gr, gn, batch)
        o_att = _attn_prompt(qa, ka, va, batch)
        res = _post(xp, y_ret, o_att, *ffn, tm=tm, tiles_per_seq=tiles_per_seq, rows_per_seq=tm,
                    final_w=final_w if last else None)
        xp, p_conv = res[0], res[1]
        if last:
            y_prompt = res[2]
        pr.append(s_fin)
        pc.append(p_conv)

        qr, kr, vr, gr, qa, ka, va = _in_proj(xs, n1, w_in_l, tabs_s, tm=n_s, tiles_per_seq=1)
        y_ret, s_new = _ret_sample(qr, kr, vr, gr, gn, state_ret[l], dec_seq)
        o_att = _attn_sample(qa, ka, va, cache_win_k, cache_win_v, l, dec_seq)
        ctx = state_conv[l]
        zero = jnp.zeros((dec_batch, dec_seq - 1, 2 * d_ff), F32)
        prev1 = jnp.concatenate([ctx[:, 1:2], zero], axis=1).reshape(n_s, 2 * d_ff)
        prev2 = jnp.concatenate([ctx, zero[:, 1:]], axis=1).reshape(n_s, 2 * d_ff)
        res = _post(xs, y_ret, o_att, *ffn, tm=n_s, tiles_per_seq=1, rows_per_seq=dec_seq,
                    prev=(prev1, prev2), final_w=final_w if last else None)
        xs, u_s = res[0], res[1]
        if last:
            y_sample = res[2]
        sk.append(ka.reshape(dec_batch, dec_seq, N_HEADS_ATT, HD_ATT))
        sv.append(va.reshape(dec_batch, dec_seq, N_HEADS_ATT, HD_ATT))
        sr.append(s_new)
        sc.append(u_s.reshape(dec_batch, dec_seq, 2 * d_ff)[:, dec_seq - 2:])

    return (y_prompt.reshape(batch, seq, d), y_sample.reshape(dec_batch, dec_seq, d),
            win_stack[0], win_stack[1], jnp.stack(pr), jnp.stack(pc),
            jnp.stack(sk), jnp.stack(sv), jnp.stack(sr), jnp.stack(sc))
```

```python
import functools

import jax
import jax.numpy as jnp
from jax import lax
from jax.experimental import pallas as pl
from jax.experimental.pallas import tpu as pltpu

F32 = jnp.float32
BF16 = jnp.bfloat16

PAST_LEN = 8192
N_HEADS_RET = 4
HD_RET = 128
N_HEADS_ATT = 8
HD_ATT = 64
D_RET = N_HEADS_RET * HD_RET
D_ATT = N_HEADS_ATT * HD_ATT
DIL_PATTERNS = ((128, 1), (512, 4), (2048, 16))
RET_CHUNK = 128
ROPE_THETA = 10000.0
EPS = 1e-6

LANES = 128
ATT_BLOCK = 128
ATT_SPAN = 2048
N_PAIRS = D_ATT // LANES
SEC = 512
VMEM_LIMIT = 56 * 1024 * 1024


def _cparams(n_axes):
    return pltpu.CompilerParams(dimension_semantics=("arbitrary",) * n_axes,
                                vmem_limit_bytes=VMEM_LIMIT)


def _const_spec(shape):
    zeros = (0,) * len(shape)
    return pl.BlockSpec(shape, lambda *_: zeros, pipeline_mode=pl.Buffered(1))


def _rms(x, w):
    ms = jnp.mean(x * x, axis=-1, keepdims=True)
    return x * lax.rsqrt(ms + EPS) * w


def _silu(g):
    return g * (1.0 / (1.0 + jnp.exp(-g)))


def _dot(a, b):
    return jnp.dot(a, b, preferred_element_type=F32)


def _dot_nt(a, b):
    return lax.dot_general(a, b, (((1,), (1,)), ((), ())), preferred_element_type=F32)


def _dot_tn(a, b):
    return lax.dot_general(a, b, (((0,), (0,)), ((), ())), preferred_element_type=F32)


def _in_proj_body(*refs, sample, tiles_per_seq, win_tiles, n_aliased):
    x_ref, nw_ref, w_ref, cr_ref, sr_ref, ca_ref, sa_ref = refs[n_aliased:n_aliased + 7]
    rest = refs[n_aliased + 7:]
    qr_ref, kr_ref, vr_ref, gr_ref, qa_ref, ka_ref, va_ref = rest[:7]
    wk_ref, wv_ref = (None, None) if sample else rest[7:9]
    tm = x_ref.shape[0]
    h = _rms(x_ref[...], nw_ref[...]).astype(BF16)
    cr, sr, ca, sa = cr_ref[...], sr_ref[...], ca_ref[...], sa_ref[...]
    lane = lax.broadcasted_iota(jnp.int32, cr.shape, 1)
    first_half = (lane % HD_ATT) < (HD_ATT // 2)

    def sec(s):
        return _dot(h, w_ref[:, s * SEC:(s + 1) * SEC])

    def rot_ret(z):
        return z * cr + pltpu.roll(z, HD_RET // 2, 1) * sr

    def rot_att(z):
        partner = jnp.where(first_half, pltpu.roll(z, LANES - HD_ATT // 2, 1),
                            pltpu.roll(z, HD_ATT // 2, 1))
        return z * ca + partner * sa

    def chunks(s, fn):
        z = sec(s)
        return [fn(z[:, c * LANES:(c + 1) * LANES]) for c in range(SEC // LANES)]

    def emit_wide(s, out_ref, fn):
        for c, v in enumerate(chunks(s, fn)):
            out_ref[:, c * LANES:(c + 1) * LANES] = v.astype(out_ref.dtype)

    def emit_att(s, out_ref, fn, win_ref=None):
        vs = chunks(s, fn)
        if sample:
            out_ref[...] = jnp.concatenate(vs, axis=-1).reshape(tm, N_HEADS_ATT, HD_ATT)
            return
        for c, v in enumerate(vs):
            out_ref[c] = v
        if win_ref is not None:
            @pl.when((pl.program_id(0) % tiles_per_seq) >= (tiles_per_seq - win_tiles))
            def _():
                win_ref[...] = jnp.concatenate(vs, axis=-1).reshape(tm, N_HEADS_ATT, HD_ATT)

    ident = lambda z: z
    emit_wide(0, qr_ref, rot_ret)
    emit_wide(1, kr_ref, rot_ret)
    emit_wide(2, vr_ref, ident)
    emit_wide(3, gr_ref, ident)
    emit_att(4, qa_ref, lambda z: rot_att(z) * (HD_ATT ** -0.5))
    emit_att(5, ka_ref, rot_att, wk_ref)
    emit_att(6, va_ref, ident, wv_ref)


def _in_proj(x, nw, w, tabs, *, tm, tiles_per_seq, win_tiles=0, win_stack=None, layer=0, depth=1):
    n, d = x.shape
    sample = win_stack is None
    aliased = (not sample) and win_stack[0] is not None
    n_tiles = n // tm
    n_seq = n_tiles // tiles_per_seq
    tab_tiles = tabs[0].shape[0] // tm
    row = lambda i: (i, 0)
    tab = lambda i: (i % tab_tiles, 0)
    ins = [x, nw, w, *tabs]
    in_specs = ([pl.BlockSpec((tm, d), row), _const_spec((1, d)), _const_spec(w.shape)]
                + [pl.BlockSpec((tm, LANES), tab)] * 4)
    wide = pl.BlockSpec((tm, SEC), row)
    if sample:
        out_shape = ([jax.ShapeDtypeStruct((n, SEC), F32)] * 4
                     + [jax.ShapeDtypeStruct((n, N_HEADS_ATT, HD_ATT), F32)] * 3)
        out_specs = [wide] * 4 + [pl.BlockSpec((tm, N_HEADS_ATT, HD_ATT), lambda i: (i, 0, 0))] * 3
        aliases = {}
    else:
        first_win = tiles_per_seq - win_tiles
        win = lambda i: (layer, i // tiles_per_seq, jnp.maximum(i % tiles_per_seq - first_win, 0), 0, 0)
        stack = jax.ShapeDtypeStruct((depth, n_seq, win_tiles * tm, N_HEADS_ATT, HD_ATT), F32)
        out_shape = ([jax.ShapeDtypeStruct((n, SEC), BF16)] * 4
                     + [jax.ShapeDtypeStruct((N_PAIRS, n, LANES), F32)] * 3 + [stack] * 2)
        out_specs = ([wide] * 4 + [pl.BlockSpec((N_PAIRS, tm, LANES), lambda i: (0, i, 0))] * 3
                     + [pl.BlockSpec((None, None, tm, N_HEADS_ATT, HD_ATT), win)] * 2)
        aliases = {}
        if aliased:
            ins = list(win_stack) + ins
            in_specs = [pl.BlockSpec(memory_space=pl.ANY)] * 2 + in_specs
            aliases = {0: 7, 1: 8}
    return pl.pallas_call(
        functools.partial(_in_proj_body, sample=sample, tiles_per_seq=tiles_per_seq, win_tiles=win_tiles,
                          n_aliased=len(aliases)),
        grid=(n_tiles,),
        in_specs=in_specs,
        out_specs=out_specs,
        out_shape=out_shape,
        input_output_aliases=aliases,
        compiler_params=_cparams(1),
        name="in_proj_sample" if sample else "in_proj",
    )(*ins)


def _gn_gate(o, g, gn_w):
    mu = jnp.mean(o, axis=-1, keepdims=True)
    d = o - mu
    var = jnp.mean(d * d, axis=-1, keepdims=True)
    y = d * lax.rsqrt(var + EPS) * gn_w
    return _silu(g.astype(F32)) * y


def _ret_prompt_body(q_ref, k_ref, v_ref, g_ref, gn_ref, dm_ref, qd_ref, kd_ref, cd_ref,
                     y_ref, sfin_ref, s_scr):
    step = pl.program_id(1)

    @pl.when(step == 0)
    def _():
        s_scr[...] = jnp.zeros_like(s_scr)

    for h in range(N_HEADS_RET):
        sl = slice(h * HD_RET, (h + 1) * HD_RET)
        state = s_scr[h]
        for c in range(q_ref.shape[0] // RET_CHUNK):
            rs = slice(c * RET_CHUNK, (c + 1) * RET_CHUNK)
            q, k, v = q_ref[rs, sl], k_ref[rs, sl], v_ref[rs, sl]
            s = _dot_nt(q, k) * dm_ref[h]
            inner = _dot(s.astype(BF16), v)
            cross = _dot(q, state.astype(BF16)) * qd_ref[h]
            kd = (k.astype(F32) * kd_ref[h]).astype(BF16)
            state = cd_ref[h] * state + _dot_tn(kd, v)
            y_ref[rs, sl] = _gn_gate(inner + cross, g_ref[rs, sl], gn_ref[:, sl]).astype(y_ref.dtype)
        s_scr[h] = state

    @pl.when(step == pl.num_programs(1) - 1)
    def _():
        sfin_ref[0] = s_scr[...]


def _ret_tables(chunk, rows_per_seq):
    lg = jnp.log1p(-jnp.exp2(-5.0 - jnp.arange(N_HEADS_RET, dtype=F32)))
    r = jnp.arange(chunk)
    t = (r % rows_per_seq).astype(F32)
    diff = t[:, None] - t[None, :]
    ok = (diff >= 0) & ((r // rows_per_seq)[:, None] == (r // rows_per_seq)[None, :])
    scale = HD_RET ** -0.5
    dm = jnp.where(ok[None], jnp.exp(jnp.where(ok, diff, 0.0)[None] * lg[:, None, None]), 0.0) * scale
    qd = jnp.exp((t + 1.0)[None, :] * lg[:, None])
    kd = jnp.exp((rows_per_seq - 1.0 - t)[None, :] * lg[:, None]) * scale
    cd = jnp.exp(rows_per_seq * lg)
    bc = lambda a: jnp.broadcast_to(a[:, :, None], (N_HEADS_RET, chunk, HD_RET))
    return dm, bc(qd), bc(kd), jnp.broadcast_to(cd[:, None, None], (N_HEADS_RET, HD_RET, HD_RET))


def _ret_prompt(qr, kr, vr, gr, gn_w, batch):
    n = qr.shape[0]
    rows = 4 * RET_CHUNK
    steps = n // batch // rows
    tabs = _ret_tables(RET_CHUNK, RET_CHUNK)
    blk = pl.BlockSpec((rows, D_RET), lambda b, c: (b * steps + c, 0))
    st_shape = (N_HEADS_RET, HD_RET, HD_RET)
    return pl.pallas_call(
        _ret_prompt_body,
        grid=(batch, steps),
        in_specs=[blk] * 4 + [_const_spec((1, D_RET))] + [_const_spec(t.shape) for t in tabs],
        out_specs=[blk, pl.BlockSpec((1,) + st_shape, lambda b, c: (b, 0, 0, 0))],
        out_shape=[jax.ShapeDtypeStruct((n, D_RET), BF16),
                   jax.ShapeDtypeStruct((batch,) + st_shape, F32)],
        scratch_shapes=[pltpu.VMEM(st_shape, F32)],
        compiler_params=_cparams(2),
        name="ret_prompt",
    )(qr, kr, vr, gr, gn_w, *tabs)


def _ret_sample_body(q_ref, k_ref, v_ref, g_ref, gn_ref, dm_ref, qd_ref, kd_ref, cd_ref, s0_ref,
                     y_ref, s1_ref, *, rows_per_seq):
    rows = q_ref.shape[0]
    seqs = rows // rows_per_seq
    seq_of_row = lax.broadcasted_iota(jnp.int32, (rows, HD_RET), 0) // rows_per_seq
    for h in range(N_HEADS_RET):
        sl = slice(h * HD_RET, (h + 1) * HD_RET)
        q, k, v = q_ref[:, sl], k_ref[:, sl], v_ref[:, sl]
        qb, vb = q.astype(BF16), v.astype(BF16)
        s = _dot_nt(qb, k.astype(BF16)) * dm_ref[h]
        inner = _dot(s.astype(BF16), vb)
        kd = k * kd_ref[h]
        cross = jnp.zeros_like(inner)
        for b in range(seqs):
            mine = seq_of_row == b
            state = s0_ref[b, h]
            cross = jnp.where(mine, _dot(qb, state.astype(BF16)), cross)
            kd_b = jnp.where(mine, kd, 0.0).astype(BF16)
            s1_ref[b, h] = cd_ref[h] * state + _dot_tn(kd_b, vb)
        y_ref[:, sl] = _gn_gate(inner + cross * qd_ref[h], g_ref[:, sl], gn_ref[:, sl]).astype(y_ref.dtype)


def _ret_sample(qr, kr, vr, gr, gn_w, s0, rows_per_seq):
    n = qr.shape[0]
    rows = 16
    seqs = rows // rows_per_seq
    tabs = _ret_tables(rows, rows_per_seq)
    blk = pl.BlockSpec((rows, D_RET), lambda i: (i, 0))
    st_blk = pl.BlockSpec((seqs, N_HEADS_RET, HD_RET, HD_RET), lambda i: (i, 0, 0, 0))
    return pl.pallas_call(
        functools.partial(_ret_sample_body, rows_per_seq=rows_per_seq),
        grid=(n // rows,),
        in_specs=[blk] * 4 + [_const_spec((1, D_RET))] + [_const_spec(t.shape) for t in tabs] + [st_blk],
        out_specs=[blk, st_blk],
        out_shape=[jax.ShapeDtypeStruct((n, D_RET), BF16), jax.ShapeDtypeStruct(s0.shape, F32)],
        compiler_params=_cparams(1),
        name="ret_sample",
    )(qr, kr, vr, gr, gn_w, *tabs, s0)


def _attn_prompt_body(q_ref, k_ref, v_ref, o_ref, kring, vring, m_s, l_s, acc_s):
    span = pl.program_id(2)
    ring_rows = kring.shape[0]
    cur_base = pl.multiple_of((span % 2) * ATT_SPAN, ATT_SPAN)
    kring[pl.ds(cur_base, ATT_SPAN), :] = k_ref[...]
    vring[pl.ds(cur_base, ATT_SPAN), :] = v_ref[...]

    @pl.when(span == 0)
    def _():
        kring[ATT_SPAN:, :] = jnp.zeros((ATT_SPAN, LANES), F32)
        vring[ATT_SPAN:, :] = jnp.zeros((ATT_SPAN, LANES), F32)

    row = lax.broadcasted_iota(jnp.int32, (ATT_BLOCK, ATT_BLOCK), 0)
    col = lax.broadcasted_iota(jnp.int32, (ATT_BLOCK, ATT_BLOCK), 1)
    tri = col <= row
    lane_lo = lax.broadcasted_iota(jnp.int32, (ATT_BLOCK, LANES), 1) < HD_ATT
    neg = -jnp.inf

    def rows(start, dil):
        if dil == 1:
            return pl.ds(pl.multiple_of(start, ATT_BLOCK), ATT_BLOCK)
        return pl.ds(start, ATT_BLOCK, stride=dil)

    def block(local, dil, no_prev):
        cur = cur_base + local
        prev = (cur - ATT_BLOCK * dil) & (ring_rows - 1)
        q = q_ref[rows(local, dil), :].astype(BF16)
        kc, vc = kring[rows(cur, dil), :], vring[rows(cur, dil), :]
        kp, vp = kring[rows(prev, dil), :], vring[rows(prev, dil), :]
        kk = jnp.concatenate([kp, kc], axis=0).astype(BF16)
        vv = jnp.concatenate([vp, vc], axis=0).astype(BF16)
        kp_r, vp_r = kp.astype(BF16).astype(F32), vp.astype(BF16).astype(F32)
        m_pair = l_pair = o_pair = None
        for hh in range(2):
            mine = lane_lo if hh == 0 else jnp.logical_not(lane_lo)
            qm = jnp.where(mine, q, jnp.zeros_like(q))
            s_full = _dot_nt(qm, kk)
            sp = jnp.where(no_prev, neg, s_full[:, :ATT_BLOCK])
            sc = s_full[:, ATT_BLOCK:]
            dg = jnp.where(no_prev, neg, jnp.sum(qm.astype(F32) * kp_r, axis=-1, keepdims=True))
            s = jnp.where(tri, sc, sp)
            m = jnp.maximum(jnp.max(s, axis=-1, keepdims=True), dg)
            p = jnp.exp(s - m)
            pd = jnp.exp(dg - m)
            den = jnp.sum(p, axis=-1, keepdims=True) + pd
            pcat = jnp.concatenate([jnp.where(tri, 0.0, p), jnp.where(tri, p, 0.0)], axis=1)
            o = _dot(pcat.astype(BF16), vv) + pd * vp_r
            mb, lb = jnp.broadcast_to(m, o.shape), jnp.broadcast_to(den, o.shape)
            m_pair = mb if hh == 0 else jnp.where(mine, mb, m_pair)
            l_pair = lb if hh == 0 else jnp.where(mine, lb, l_pair)
            o_pair = o if hh == 0 else jnp.where(mine, o, o_pair)
        return m_pair, l_pair, o_pair

    def merged(sel, new):
        m2, l2, o2 = new
        m1, l1, o1 = m_s[sel, :], l_s[sel, :], acc_s[sel, :]
        m = jnp.maximum(m1, m2)
        a1, a2 = jnp.exp(m1 - m), jnp.exp(m2 - m)
        return m, a1 * l1 + a2 * l2, a1 * o1 + a2 * o2

    first_span = span == 0
    n_blocks = ATT_SPAN // ATT_BLOCK

    def wide_body(i, carry):
        sel = rows(i, 16)
        m, l, o = block(i, 16, first_span)
        m_s[sel, :], l_s[sel, :], acc_s[sel, :] = m, l, o
        return carry

    def mid_body(i, carry):
        local = (i >> 2) * (ATT_BLOCK * 4) + (i & 3)
        sel = rows(local, 4)
        m, l, o = merged(sel, block(local, 4, jnp.logical_and(first_span, i < 4)))
        m_s[sel, :], l_s[sel, :], acc_s[sel, :] = m, l, o
        return carry

    def near_body(i, carry):
        local = pl.multiple_of(i * ATT_BLOCK, ATT_BLOCK)
        sel = rows(local, 1)
        _, l, o = merged(sel, block(local, 1, jnp.logical_and(first_span, i == 0)))
        o_ref[sel, :] = (o * (1.0 / l)).astype(o_ref.dtype)
        return carry

    lax.fori_loop(0, n_blocks, wide_body, 0, unroll=8)
    lax.fori_loop(0, n_blocks, mid_body, 0, unroll=8)
    lax.fori_loop(0, n_blocks, near_body, 0, unroll=8)


def _attn_prompt(qa, ka, va, batch):
    n = qa.shape[1]
    spans = n // batch // ATT_SPAN
    assert [d for _, d in DIL_PATTERNS] == [1, 4, 16] and all(w == ATT_BLOCK * d for w, d in DIL_PATTERNS)
    slab = pl.BlockSpec((None, ATT_SPAN, LANES), lambda b, p, s: (p, b * spans + s, 0))
    return pl.pallas_call(
        _attn_prompt_body,
        grid=(batch, N_PAIRS, spans),
        in_specs=[slab] * 3,
        out_specs=pl.BlockSpec((ATT_SPAN, LANES), lambda b, p, s: (b * spans + s, p)),
        out_shape=jax.ShapeDtypeStruct((n, D_ATT), BF16),
        scratch_shapes=[pltpu.VMEM((2 * ATT_SPAN, LANES), F32)] * 2 + [pltpu.VMEM((ATT_SPAN, LANES), F32)] * 3,
        compiler_params=_cparams(3),
        name="attn_prompt",
    )(qa, ka, va)


def _attn_sample_body(q_ref, kn_ref, vn_ref, kf_ref, vf_ref, kt_ref, vt_ref, o_ref, *, n_new, n_ctx):
    b = pl.program_id(0)
    widest = max(d for _, d in DIL_PATTERNS)
    tail_rows = kt_ref.shape[0]
    tail_start = n_ctx - tail_rows
    rows = n_new * N_HEADS_ATT
    mat = lambda ref_val: ref_val.reshape(-1, HD_ATT).astype(BF16)
    q = mat(q_ref[pl.ds(b * n_new, n_new)])
    k_new, v_new = mat(kn_ref[pl.ds(b * n_new, n_new)]), mat(vn_ref[pl.ds(b * n_new, n_new)])
    k_far, v_far, k_tail, v_tail = mat(kf_ref[...]), mat(vf_ref[...]), mat(kt_ref[...]), mat(vt_ref[...])

    def weight(cols, pos_of_col):
        r = lax.broadcasted_iota(jnp.int32, (rows, cols), 0)
        c = lax.broadcasted_iota(jnp.int32, (rows, cols), 1)
        pos, valid = pos_of_col(c // N_HEADS_ATT)
        delta = n_ctx + r // N_HEADS_ATT - pos
        w = jnp.zeros((rows, cols), F32)
        for window, dil in DIL_PATTERNS:
            w = w + jnp.where((delta >= 0) & (delta % dil == 0) & (delta <= window), 1.0, 0.0)
        return jnp.where(valid & (c % N_HEADS_ATT == r % N_HEADS_ATT), w, 0.0)

    def far_pos(g):
        pos = (g // n_new) * widest + g % n_new
        return pos, pos < tail_start

    parts = [
        (k_far, v_far, weight(k_far.shape[0], far_pos)),
        (k_tail, v_tail, weight(k_tail.shape[0], lambda g: (tail_start + g, g >= 0))),
        (k_new, v_new, weight(k_new.shape[0], lambda g: (n_ctx + g, g >= 0))),
    ]
    neg = -jnp.inf
    scores = [jnp.where(w > 0, _dot_nt(q, k), neg) for k, _, w in parts]
    m = functools.reduce(jnp.maximum, [jnp.max(s, axis=-1, keepdims=True) for s in scores])
    ps = [w * jnp.exp(s - m) for s, (_, _, w) in zip(scores, parts)]
    den = functools.reduce(jnp.add, [jnp.sum(p, axis=-1, keepdims=True) for p in ps])
    o = functools.reduce(jnp.add, [_dot(p.astype(BF16), v) for p, (_, v, _) in zip(ps, parts)])
    o_ref[pl.ds(b * n_new, n_new)] = (o * (1.0 / den)).reshape(n_new, N_HEADS_ATT, HD_ATT)


def _attn_sample(qa, ka, va, cache_k, cache_v, layer, n_new):
    n = qa.shape[0]
    depth, batch, n_ctx = cache_k.shape[:3]
    widest = max(d for _, d in DIL_PATTERNS)
    tail = max(w for w, d in DIL_PATTERNS if d < widest)
    assert n_new <= widest and n_ctx % widest == 0 and n_ctx % tail == 0
    full = pl.BlockSpec((n, N_HEADS_ATT, HD_ATT), lambda b: (0, 0, 0), pipeline_mode=pl.Buffered(1))
    grouped = lambda c: c.reshape(depth, batch, n_ctx // widest, widest, N_HEADS_ATT, HD_ATT)
    far = pl.BlockSpec((None, None, n_ctx // widest, n_new, N_HEADS_ATT, HD_ATT),
                       lambda b: (layer, b, 0, 0, 0, 0))
    last = pl.BlockSpec((None, None, tail, N_HEADS_ATT, HD_ATT), lambda b: (layer, b, n_ctx // tail - 1, 0, 0))
    return pl.pallas_call(
        functools.partial(_attn_sample_body, n_new=n_new, n_ctx=n_ctx),
        grid=(batch,),
        in_specs=[full, full, full, far, far, last, last],
        out_specs=pl.BlockSpec((n, N_HEADS_ATT, HD_ATT), lambda b: (0, 0, 0)),
        out_shape=jax.ShapeDtypeStruct((n, N_HEADS_ATT, HD_ATT), F32),
        compiler_params=_cparams(1),
        name="attn_sample",
    )(qa, ka, va, grouped(cache_k), grouped(cache_v), cache_k, cache_v)


def _post_body(*refs, sample, rows_per_seq, tiles_per_seq, final, d_ff, chunk):
    it = iter(refs)
    x_ref, y_ref, o_ref = next(it), next(it), next(it)
    wout_ref, n2_ref, wup_ref, cw_ref, cb_ref, wdn_ref = (next(it) for _ in range(6))
    prev1_ref, prev2_ref = (next(it), next(it)) if sample else (None, None)
    fw_ref = next(it) if final else None
    xo_ref, u_ref = next(it), next(it)
    yfin_ref = next(it) if final else None
    carry = None if sample else next(it)

    tm = x_ref.shape[0]
    o_att = o_ref[...].reshape(tm, D_ATT) if sample else o_ref[...]
    mix = _dot(y_ref[...], wout_ref[:D_RET, :]) + _dot(o_att.astype(BF16), wout_ref[D_RET:, :])
    x1 = x_ref[...] + mix
    h2 = _rms(x1, n2_ref[...]).astype(BF16)

    row = lax.broadcasted_iota(jnp.int32, (tm, chunk), 0)
    t = row % rows_per_seq
    if not sample:
        @pl.when(pl.program_id(0) % tiles_per_seq == 0)
        def _():
            carry[...] = jnp.zeros_like(carry)

    def conv(u, cs):
        if sample:
            p1, p2 = prev1_ref[:, cs], prev2_ref[:, cs]
            u_ref[:, cs] = u
        else:
            c6, c7 = carry[6:7, cs], carry[7:8, cs]
            p1 = jnp.broadcast_to(c7, u.shape)
            p2 = jnp.where(row == 0, c6, c7)
            carry[:, cs] = u[tm - 8:, :]
        s1 = jnp.where(t >= 1, pltpu.roll(u, 1, 0), p1)
        s2 = jnp.where(t >= 2, pltpu.roll(u, 2, 0), p2)
        return ((cb_ref[:, cs] + s2 * cw_ref[0:1, cs]) + s1 * cw_ref[1:2, cs]) + u * cw_ref[2:3, cs]

    acc = jnp.zeros((tm, xo_ref.shape[1]), F32)
    for c in range(d_ff // chunk):
        gs = slice(c * chunk, (c + 1) * chunk)
        vs = slice(d_ff + c * chunk, d_ff + (c + 1) * chunk)
        cg = conv(_dot(h2, wup_ref[:, gs]), gs)
        cv = conv(_dot(h2, wup_ref[:, vs]), vs)
        acc = acc + _dot((_silu(cg) * cv).astype(BF16), wdn_ref[gs, :])
    xo = x1 + acc
    xo_ref[...] = xo
    if not sample:
        u_ref[0] = carry[6:8, :]
    if final:
        yfin_ref[...] = _rms(xo, fw_ref[...])


def _post(x, y, o_att, wout, n2, wup, cw, cb, wdn, *, tm, tiles_per_seq, rows_per_seq,
          prev=None, final_w=None):
    n, d = x.shape
    d_ff = wdn.shape[0]
    sample = prev is not None
    final = final_w is not None
    n_tiles = n // tm
    row = lambda i: (i, 0)
    rblk = lambda w: pl.BlockSpec((tm, w), row)
    o_spec = pl.BlockSpec((tm, N_HEADS_ATT, HD_ATT), lambda i: (i, 0, 0)) if sample else rblk(D_ATT)
    ins = [x, y, o_att, wout, n2, wup, cw, cb, wdn]
    specs = [rblk(d), rblk(D_RET), o_spec] + [_const_spec(a.shape) for a in (wout, n2, wup, cw, cb, wdn)]
    if sample:
        ins += list(prev)
        specs += [rblk(2 * d_ff)] * 2
    if final:
        ins.append(final_w)
        specs.append(_const_spec(final_w.shape))
    outs = [jax.ShapeDtypeStruct((n, d), F32)]
    ospecs = [rblk(d)]
    if sample:
        outs.append(jax.ShapeDtypeStruct((n, 2 * d_ff), F32))
        ospecs.append(rblk(2 * d_ff))
        scratch = []
    else:
        outs.append(jax.ShapeDtypeStruct((n_tiles // tiles_per_seq, 2, 2 * d_ff), F32))
        ospecs.append(pl.BlockSpec((1, 2, 2 * d_ff), lambda i: (i // tiles_per_seq, 0, 0)))
        scratch = [pltpu.VMEM((8, 2 * d_ff), F32)]
    if final:
        outs.append(jax.ShapeDtypeStruct((n, d), F32))
        ospecs.append(rblk(d))
    return pl.pallas_call(
        functools.partial(_post_body, sample=sample, rows_per_seq=rows_per_seq,
                          tiles_per_seq=tiles_per_seq, final=final, d_ff=d_ff, chunk=256),
        grid=(n_tiles,),
        in_specs=specs,
        out_specs=ospecs,
        out_shape=outs,
        scratch_shapes=scratch,
        compiler_params=_cparams(1),
        name="post_sample" if sample else "post_prompt",
    )(*ins)


def _rope_tables(pos):
    def tab(half):
        inv = ROPE_THETA ** (-jnp.arange(half, dtype=F32) / half)
        ang = pos.astype(F32)[:, None] * inv[None, :]
        reps = LANES // (2 * half)
        cos = jnp.tile(jnp.concatenate([jnp.cos(ang)] * 2, axis=-1), (1, reps))
        sin = jnp.tile(jnp.concatenate([-jnp.sin(ang), jnp.sin(ang)], axis=-1), (1, reps))
        return cos, sin
    cr, sr = tab(HD_RET // 2)
    ca, sa = tab(HD_ATT // 2)
    return cr, sr, ca, sa


def kernel(x_prompt, x_sample, cache_win_k, cache_win_v, state_ret, state_conv, norm1_w, w_in, ret_gn_w,
           w_out, norm2_w, w_up, conv_w, conv_b, w_down, final_norm_w):
    batch, seq, d = x_prompt.shape
    dec_batch, dec_seq, _ = x_sample.shape
    depth = w_in.shape[0]
    d_ff = w_down.shape[1]
    w_keep = min(max(w for w, _ in DIL_PATTERNS), seq)
    tm = 512
    tiles_per_seq = seq // tm
    n_s = dec_batch * dec_seq

    tabs_p = _rope_tables(jnp.arange(seq, dtype=jnp.int32))
    tabs_s = _rope_tables(PAST_LEN + (jnp.arange(n_s, dtype=jnp.int32) % dec_seq))
    row2 = lambda a: a.reshape(1, -1)
    final_w = row2(final_norm_w)

    xp = x_prompt.reshape(batch * seq, d)
    xs = x_sample.reshape(n_s, d)
    win_stack = (None, None)
    pr, pc, sk, sv, sr, sc = ([] for _ in range(6))
    y_prompt = y_sample = None
    for l in range(depth):
        last = l == depth - 1
        w_in_l, w_out_l = w_in[l].astype(BF16), w_out[l].astype(BF16)
        w_up_l, w_dn_l = w_up[l].astype(BF16), w_down[l].astype(BF16)
        n1, n2, gn, cb = row2(norm1_w[l]), row2(norm2_w[l]), row2(ret_gn_w[l]), row2(conv_b[l])
        ffn = (w_out_l, n2, w_up_l, conv_w[l], cb, w_dn_l)

        qr, kr, vr, gr, qa, ka, va, wk, wv = _in_proj(
            xp, n1, w_in_l, tabs_p, tm=tm, tiles_per_seq=tiles_per_seq, win_tiles=w_keep // tm,
            win_stack=win_stack, layer=l, depth=depth)
        win_stack = (wk, wv)
        y_ret, s_fin = _ret_prompt(qr, kr, vr, gr, gn, batch)
        o_att = _attn_prompt(qa, ka, va, batch)
        res = _post(xp, y_ret, o_att, *ffn, tm=tm, tiles_per_seq=tiles_per_seq, rows_per_seq=tm,
                    final_w=final_w if last else None)
        xp, p_conv = res[0], res[1]
        if last:
            y_prompt = res[2]
        pr.append(s_fin)
        pc.append(p_conv)

        qr, kr, vr, gr, qa, ka, va = _in_proj(xs, n1, w_in_l, tabs_s, tm=n_s, tiles_per_seq=1)
        y_ret, s_new = _ret_sample(qr, kr, vr, gr, gn, state_ret[l], dec_seq)
        o_att = _attn_sample(qa, ka, va, cache_win_k, cache_win_v, l, dec_seq)
        ctx = state_conv[l]
        zero = jnp.zeros((dec_batch, dec_seq - 1, 2 * d_ff), F32)
        prev1 = jnp.concatenate([ctx[:, 1:2], zero], axis=1).reshape(n_s, 2 * d_ff)
        prev2 = jnp.concatenate([ctx, zero[:, 1:]], axis=1).reshape(n_s, 2 * d_ff)
        res = _post(xs, y_ret, o_att, *ffn, tm=n_s, tiles_per_seq=1, rows_per_seq=dec_seq,
                    prev=(prev1, prev2), final_w=final_w if last else None)
        xs, u_s = res[0], res[1]
        if last:
            y_sample = res[2]
        sk.append(ka.reshape(dec_batch, dec_seq, N_HEADS_ATT, HD_ATT))
        sv.append(va.reshape(dec_batch, dec_seq, N_HEADS_ATT, HD_ATT))
        sr.append(s_new)
        sc.append(u_s.reshape(dec_batch, dec_seq, 2 * d_ff)[:, dec_seq - 2:])

    return (y_prompt.reshape(batch, seq, d), y_sample.reshape(dec_batch, dec_seq, d),
            win_stack[0], win_stack[1], jnp.stack(pr), jnp.stack(pc),
            jnp.stack(sk), jnp.stack(sv), jnp.stack(sr), jnp.stack(sc))
```
